```python
import jax, jax.numpy as jnp
from jax import lax
import numpy as np

D_MODEL = 1024
BATCH = 16
SEQ = 2048
DEPTH = 4

N_META = 16
D_MIX = D_MODEL
D_CONV = D_MIX // 4
D_LRU = D_MIX // 2
D_POOL = D_MIX // 4
N_LRU_HEADS = 8
LRU_HEAD_DIM = D_LRU // N_LRU_HEADS
LRU_C = 8.0
CONV_A_WIDTH = 31
CONV_B_WIDTH = 4
POOL_WINDOWS = (2, 4, 8, 16)
N_POOL_GROUPS = len(POOL_WINDOWS)
POOL_GROUP_DIM = D_POOL // N_POOL_GROUPS
D_FF = 2816
CONV_F_WIDTH = 3
D_IN = 2 * D_CONV + 2 * D_LRU + D_POOL
ALPHA = (2.0 * DEPTH) ** 0.25
BETA = (8.0 * DEPTH) ** -0.25
LN_EPS = 1e-5

kernel_name = 'hybrid_bidir_conv_lru_pool_encoder'


def layer_norm(x, g, b):
    xf = x.astype(jnp.float32)
    mu = jnp.mean(xf, axis=-1, keepdims=True)
    var = jnp.mean(jnp.square(xf - mu), axis=-1, keepdims=True)
    y = (xf - mu) * lax.rsqrt(var + LN_EPS)
    return (y * g.astype(jnp.float32) + b.astype(jnp.float32)).astype(x.dtype)


def depthwise_conv(x, w, b, pad):
    c = x.shape[-1]
    y = lax.conv_general_dilated(x, w.astype(x.dtype)[:, None, :], window_strides=(1,), padding=[pad],
                                 dimension_numbers=('NWC', 'WIO', 'NWC'), feature_group_count=c)
    return y + b.astype(x.dtype)


def block_diag(x, w):
    h = w.shape[0]
    xs = x.reshape(x.shape[:-1] + (h, -1))
    y = jnp.einsum('blhi,hij->blhj', xs, w.astype(x.dtype))
    return y.reshape(x.shape[:-1] + (-1,))


def _lin_combine(c1, c2):
    a1, b1 = c1
    a2, b2 = c2
    return a1 * a2, a2 * b1 + b2


def rglru(x, w_a, b_a, w_x, b_x, lam, reverse):
    xf = x.astype(jnp.float32)
    r = jax.nn.sigmoid(block_diag(xf, w_a) + b_a.astype(jnp.float32))
    i = jax.nn.sigmoid(block_diag(xf, w_x) + b_x.astype(jnp.float32))
    log_a = -LRU_C * r * jax.nn.softplus(-lam.astype(jnp.float32))
    a = jnp.exp(log_a)
    b = jnp.sqrt(-jnp.expm1(2.0 * log_a)) * (i * xf)
    _, h = lax.associative_scan(_lin_combine, (a, b), axis=1, reverse=reverse)
    return h.astype(x.dtype)


def multiscale_pool(v):
    vf = v.astype(jnp.float32)
    nb, L, _ = vf.shape
    cs = jnp.concatenate([jnp.zeros((nb, 1, D_POOL), jnp.float32), jnp.cumsum(vf, axis=1)], axis=1)
    t = np.arange(L)
    outs = []
    for g, w in enumerate(POOL_WINDOWS):
        lo = np.clip(t - w // 2, 0, L)
        hi = np.clip(t + w // 2, 0, L)
        csg = cs[..., g * POOL_GROUP_DIM:(g + 1) * POOL_GROUP_DIM]
        s = jnp.take(csg, hi, axis=1) - jnp.take(csg, lo, axis=1)
        cnt = jnp.asarray((hi - lo).astype(np.float32))[None, :, None]
        outs.append(s / cnt)
    return (jnp.concatenate(outs, axis=-1) - vf).astype(v.dtype)


def token_mixer(h, w_in, b_in, conv_a_w, conv_a_b, norm_a_g, norm_a_b, conv_b_w, conv_b_b,
                gate_a_w, gate_a_b, gate_x_w, gate_x_b, lru_lambda, pool_w, pool_b, pool_scale,
                w_out, b_out):
    proj = h @ w_in.astype(h.dtype) + b_in.astype(h.dtype)
    u_a, g_a, y_b, x_b, v_c = jnp.split(
        proj, [D_CONV, 2 * D_CONV, 2 * D_CONV + D_LRU, 2 * D_CONV + 2 * D_LRU], axis=-1)
    pa = (CONV_A_WIDTH - 1) // 2
    a = u_a * jax.nn.sigmoid(g_a)
    a = depthwise_conv(a, conv_a_w, conv_a_b, (pa, pa))
    a = jax.nn.silu(layer_norm(a, norm_a_g, norm_a_b))
    kb = CONV_B_WIDTH - 1
    xf = depthwise_conv(x_b, conv_b_w[0], conv_b_b[0], (kb, 0))
    xr = depthwise_conv(x_b, conv_b_w[1], conv_b_b[1], (0, kb))
    h_f = rglru(xf, gate_a_w[0], gate_a_b[0], gate_x_w[0], gate_x_b[0], lru_lambda[0], False)
    h_r = rglru(xr, gate_a_w[1], gate_a_b[1], gate_x_w[1], gate_x_b[1], lru_lambda[1], True)
    b = jax.nn.gelu(y_b) * (h_f + h_r)
    c = block_diag(multiscale_pool(v_c), pool_w) + pool_b.astype(h.dtype)
    c = c * pool_scale.astype(h.dtype)
    mix = jnp.concatenate([a, b, c], axis=-1)
    return mix @ w_out.astype(h.dtype) + b_out.astype(h.dtype)


def conv_glu(h, w_up, conv_f_w, conv_f_b, w_down):
    up = h @ w_up.astype(h.dtype)
    gate, val = jnp.split(up, 2, axis=-1)
    pf = (CONV_F_WIDTH - 1) // 2
    gate = depthwise_conv(gate, conv_f_w, conv_f_b, (pf, pf))
    return (jax.nn.gelu(gate) * val) @ w_down.astype(h.dtype)


def setup_inputs(seed: int = 0) -> dict:
    key = jax.random.key(seed)
    ks = jax.random.split(key, 32)
    f32 = jnp.float32

    def nrm(k, shape, std):
        return jax.random.normal(k, shape, f32) * std

    def xavier(fan_in, fan_out):
        return (2.0 / (fan_in + fan_out)) ** 0.5

    u = jax.random.uniform(ks[14], (DEPTH, 2, D_LRU), f32, minval=0.9, maxval=0.999)
    s = u ** (1.0 / LRU_C)
    lru_lambda = jnp.log(s) - jnp.log1p(-s)
    return {
        'x': nrm(ks[0], (BATCH, SEQ, D_MODEL), 1.0),
        'meta_tokens': nrm(ks[1], (N_META, D_MODEL), 1.0),
        'emb_ln_g': 1.0 + nrm(ks[2], (D_MODEL,), 0.02),
        'emb_ln_b': nrm(ks[3], (D_MODEL,), 0.02),
        'w_in': nrm(ks[4], (DEPTH, D_MODEL, D_IN), xavier(D_MODEL, D_IN)),
        'b_in': nrm(ks[5], (DEPTH, D_IN), 0.02),
        'conv_a_w': nrm(ks[6], (DEPTH, CONV_A_WIDTH, D_CONV), CONV_A_WIDTH ** -0.5),
        'conv_a_b': nrm(ks[7], (DEPTH, D_CONV), 0.02),
        'norm_a_g': 1.0 + nrm(ks[8], (DEPTH, D_CONV), 0.02),
        'norm_a_b': nrm(ks[9], (DEPTH, D_CONV), 0.02),
        'conv_b_w': nrm(ks[10], (DEPTH, 2, CONV_B_WIDTH, D_LRU), CONV_B_WIDTH ** -0.5),
        'conv_b_b': nrm(ks[11], (DEPTH, 2, D_LRU), 0.02),
        'gate_a_w': nrm(ks[12], (DEPTH, 2, N_LRU_HEADS, LRU_HEAD_DIM, LRU_HEAD_DIM), LRU_HEAD_DIM ** -0.5),
        'gate_a_b': nrm(ks[13], (DEPTH, 2, D_LRU), 0.02),
        'gate_x_w': nrm(ks[15], (DEPTH, 2, N_LRU_HEADS, LRU_HEAD_DIM, LRU_HEAD_DIM), LRU_HEAD_DIM ** -0.5),
        'gate_x_b': nrm(ks[16], (DEPTH, 2, D_LRU), 0.02),
        'lru_lambda': lru_lambda,
        'pool_w': nrm(ks[17], (DEPTH, N_POOL_GROUPS, POOL_GROUP_DIM, POOL_GROUP_DIM), POOL_GROUP_DIM ** -0.5),
        'pool_b': nrm(ks[18], (DEPTH, D_POOL), 0.02),
        'pool_scale': 1.0 + nrm(ks[19], (DEPTH, D_POOL), 0.1),
        'w_out': nrm(ks[20], (DEPTH, D_MIX, D_MODEL), BETA * xavier(D_MIX, D_MODEL)),
        'b_out': nrm(ks[21], (DEPTH, D_MODEL), 0.02),
        'ln1_g': 1.0 + nrm(ks[22], (DEPTH, D_MODEL), 0.02),
        'ln1_b': nrm(ks[23], (DEPTH, D_MODEL), 0.02),
        'w_up': nrm(ks[24], (DEPTH, D_MODEL, 2 * D_FF), xavier(D_MODEL, 2 * D_FF)),
        'conv_f_w': nrm(ks[25], (DEPTH, CONV_F_WIDTH, D_FF), CONV_F_WIDTH ** -0.5),
        'conv_f_b': nrm(ks[26], (DEPTH, D_FF), 0.02),
        'w_down': nrm(ks[27], (DEPTH, D_FF, D_MODEL), BETA * xavier(D_FF, D_MODEL)),
        'ln2_g': 1.0 + nrm(ks[28], (DEPTH, D_MODEL), 0.02),
        'ln2_b': nrm(ks[29], (DEPTH, D_MODEL), 0.02),
    }


def reference(x, meta_tokens, emb_ln_g, emb_ln_b, w_in, b_in, conv_a_w, conv_a_b, norm_a_g, norm_a_b,
              conv_b_w, conv_b_b, gate_a_w, gate_a_b, gate_x_w, gate_x_b, lru_lambda, pool_w, pool_b,
              pool_scale, w_out, b_out, ln1_g, ln1_b, w_up, conv_f_w, conv_f_b, w_down, ln2_g, ln2_b):
    nb = x.shape[0]
    meta = jnp.broadcast_to(meta_tokens.astype(x.dtype)[None], (nb, N_META, D_MODEL))
    h = jnp.concatenate([meta, x], axis=1)
    h = layer_norm(h, emb_ln_g, emb_ln_b)
    for l in range(DEPTH):
        m = token_mixer(h, w_in[l], b_in[l], conv_a_w[l], conv_a_b[l], norm_a_g[l], norm_a_b[l],
                        conv_b_w[l], conv_b_b[l], gate_a_w[l], gate_a_b[l], gate_x_w[l], gate_x_b[l],
                        lru_lambda[l], pool_w[l], pool_b[l], pool_scale[l], w_out[l], b_out[l])
        h = layer_norm(ALPHA * h + m, ln1_g[l], ln1_b[l])
        f = conv_glu(h, w_up[l], conv_f_w[l], conv_f_b[l], w_down[l])
        h = layer_norm(ALPHA * h + f, ln2_g[l], ln2_b[l])
    return h[:, N_META:]
```

```python
import functools

import jax
import jax.numpy as jnp
from jax import lax
from jax.experimental import pallas as pl
from jax.experimental.pallas import tpu as pltpu

D_MODEL = 1024
SEQ = 2048
DEPTH = 4
N_META = 16
L_TOT = SEQ + N_META
D_CONV = 256
D_LRU = 512
D_POOL = 256
N_LRU_HEADS = 8
LRU_C = 8.0
CONV_A_WIDTH = 31
CONV_B_WIDTH = 4
POOL_GROUP_DIM = 64
D_FF = 2816
D_IN = 2 * D_CONV + 2 * D_LRU + D_POOL
ALPHA = (2.0 * DEPTH) ** 0.25
LN_EPS = 1e-5

COL_UG = 0
COL_Y = 2 * D_CONV
COL_X = 2 * D_CONV + D_LRU
COL_V = 2 * D_CONV + 2 * D_LRU

ROWS_EW = 48
ROWS_MM = 688
HALO = 16
LANE_BLK = 256
FF_BLK = 256
N_FF_BLK = D_FF // FF_BLK
VMEM_LIMIT = 60 * 1024 * 1024

F32 = jnp.float32
BF16 = jnp.bfloat16


def _row_loop(n_rows, tile, fn):
    def body(i, c):
        fn(pl.multiple_of(i * tile, tile))
        return c
    lax.fori_loop(0, n_rows // tile, body, 0)


def _layer_norm(x, g, b):
    mu = jnp.mean(x, axis=-1, keepdims=True)
    xc = x - mu
    var = jnp.mean(xc * xc, axis=-1, keepdims=True)
    return xc * lax.rsqrt(var + LN_EPS) * g + b


def _embed_kernel(x_ref, meta_ref, g_ref, b_ref, o_ref):
    g = g_ref[...]
    b = b_ref[...]
    o_ref[0, 0:N_META, :] = _layer_norm(meta_ref[...], g, b)

    def rows(r):
        o_ref[0, pl.ds(N_META + r, 64), :] = _layer_norm(x_ref[0, pl.ds(r, 64), :], g, b)
    _row_loop(SEQ, 64, rows)


def _embed(x, meta, g, b):
    nb = x.shape[0]
    return pl.pallas_call(
        _embed_kernel,
        grid=(nb,),
        in_specs=[
            pl.BlockSpec((1, SEQ, D_MODEL), lambda i: (i, 0, 0)),
            pl.BlockSpec((N_META, D_MODEL), lambda i: (0, 0)),
            pl.BlockSpec((1, D_MODEL), lambda i: (0, 0)),
            pl.BlockSpec((1, D_MODEL), lambda i: (0, 0)),
        ],
        out_specs=pl.BlockSpec((1, L_TOT, D_MODEL), lambda i: (i, 0, 0)),
        out_shape=jax.ShapeDtypeStruct((nb, L_TOT, D_MODEL), F32),
        compiler_params=pltpu.CompilerParams(
            dimension_semantics=("arbitrary",), vmem_limit_bytes=VMEM_LIMIT),
        name="embed",
    )(x, meta, g, b)


def _mixer_kernel(h_ref, w_in_ref, b_in_ref, caw_ref, cab_ref, nag_ref, nab_ref,
                  cbw_ref, cbb_ref, wg_ref, gab_ref, gxb_ref, lam_ref,
                  pw_ref, pb_ref, ps_ref, w_out_ref, b_out_ref, lng_ref, lnb_ref,
                  o_ref, xp_ref, s_ref, mix_ref):
    R = ROWS_EW
    RM = ROWS_MM

    xp_ref[0:HALO, :] = jnp.zeros((HALO, D_LRU), F32)
    xp_ref[L_TOT + HALO:L_TOT + 2 * HALO, :] = jnp.zeros((HALO, D_LRU), F32)

    def proj(r, c0, n):
        lhs = h_ref[0, pl.ds(r, RM), :].astype(BF16)
        return (jnp.dot(lhs, w_in_ref[:, c0:c0 + n], preferred_element_type=F32)
                + b_in_ref[:, c0:c0 + n])

    def a_proj(r):
        ug = proj(r, COL_UG, 2 * D_CONV)
        xp_ref[pl.ds(HALO + r, RM), 0:D_CONV] = ug[:, :D_CONV] * jax.nn.sigmoid(ug[:, D_CONV:])
    _row_loop(L_TOT, RM, a_proj)

    def a_conv(r):
        win = xp_ref[pl.ds(r, R + 2 * HALO), 0:D_CONV]
        acc = jnp.zeros((R, D_CONV), F32) + cab_ref[...]
        for s in range(8):
            shifted = win[s:s + R + 24]
            for q in range(4):
                k = 8 * q + s - 1
                if 0 <= k < CONV_A_WIDTH:
                    acc = acc + caw_ref[k:k + 1, :] * shifted[8 * q:8 * q + R]
        y = _layer_norm(acc, nag_ref[...], nab_ref[...])
        mix_ref[pl.ds(r, R), 0:D_CONV] = (y * jax.nn.sigmoid(y)).astype(BF16)
    _row_loop(L_TOT, R, a_conv)

    def c_proj(r):
        xp_ref[pl.ds(HALO + r, RM), 0:D_POOL] = proj(r, COL_V, D_POOL)
    _row_loop(L_TOT, RM, c_proj)

    lane = lax.broadcasted_iota(jnp.int32, (1, D_POOL), 1)
    half = jnp.where(lane < 64, 1, jnp.where(lane < 128, 2, jnp.where(lane < 192, 4, 8)))

    def c_pool(r):
        win = xp_ref[pl.ds(r, R + 2 * HALO), 0:D_POOL]
        s2 = win[:-1] + win[1:]
        s4 = s2[:-2] + s2[2:]
        s8 = s4[:-4] + s4[4:]
        s16 = s8[:-8] + s8[8:]
        pooled = jnp.where(lane < 64, s2[15:15 + R],
                           jnp.where(lane < 128, s4[14:14 + R],
                                     jnp.where(lane < 192, s8[12:12 + R], s16[8:8 + R])))
        t = r + lax.broadcasted_iota(jnp.int32, (R, D_POOL), 0)
        cnt = jnp.minimum(t + half, L_TOT) - jnp.maximum(t - half, 0)
        s_ref[pl.ds(r, R), :] = pooled / cnt.astype(F32) - win[HALO:HALO + R]
    _row_loop(L_TOT, R, c_pool)

    def c_out(r):
        m = s_ref[pl.ds(r, RM), :].astype(BF16)
        c = jnp.dot(m, pw_ref[...], preferred_element_type=F32) + pb_ref[...]
        mix_ref[pl.ds(r, RM), D_CONV + D_LRU:D_MODEL] = (c * ps_ref[...]).astype(BF16)
    _row_loop(L_TOT, RM, c_out)

    def b_proj(r):
        xp_ref[pl.ds(HALO + r, RM), :] = proj(r, COL_X, D_LRU)
    _row_loop(L_TOT, RM, b_proj)

    sub = lax.broadcasted_iota(jnp.int32, (8, LANE_BLK), 0)
    n_tiles = L_TOT // R

    for d in range(2):
        for cb in range(D_LRU // LANE_BLK):
            c0 = cb * LANE_BLK
            hs0 = D_LRU + c0

            def b_conv(r, d=d, c0=c0):
                win = xp_ref[pl.ds(r + 8, R + 16), c0:c0 + LANE_BLK]
                acc = jnp.zeros((R, LANE_BLK), F32) + cbb_ref[d:d + 1, c0:c0 + LANE_BLK]
                for k in range(CONV_B_WIDTH):
                    off = 8 + (k - (CONV_B_WIDTH - 1) if d == 0 else k)
                    w_k = cbw_ref[d * CONV_B_WIDTH + k:d * CONV_B_WIDTH + k + 1, c0:c0 + LANE_BLK]
                    acc = acc + w_k * win[off:off + R]
                s_ref[pl.ds(r, R), :] = acc
            _row_loop(L_TOT, R, b_conv)

            def b_gate_dot(r, d=d, cb=cb):
                xc = s_ref[pl.ds(r, RM), :].astype(BF16)
                o_ref[0, pl.ds(r, RM), 0:2 * LANE_BLK] = jnp.dot(
                    xc, wg_ref[d, cb], preferred_element_type=F32)
            _row_loop(L_TOT, RM, b_gate_dot)

            z = -lam_ref[d:d + 1, c0:c0 + LANE_BLK]
            softplus = jnp.maximum(z, 0.0) + jnp.log1p(jnp.exp(-jnp.abs(z)))
            decay = -LRU_C * softplus

            def b_gate(r, d=d, c0=c0, decay=decay):
                rows = pl.ds(r, R)
                rg = jax.nn.sigmoid(o_ref[0, rows, 0:LANE_BLK] + gab_ref[d:d + 1, c0:c0 + LANE_BLK])
                ig = jax.nn.sigmoid(o_ref[0, rows, LANE_BLK:2 * LANE_BLK]
                                    + gxb_ref[d:d + 1, c0:c0 + LANE_BLK])
                a = jnp.exp(rg * decay)
                o_ref[0, rows, 0:LANE_BLK] = a
                o_ref[0, rows, LANE_BLK:2 * LANE_BLK] = (
                    jnp.sqrt(1.0 - a * a) * (ig * s_ref[rows, :]))
            _row_loop(L_TOT, R, b_gate)

            def b_scan(i, carry, d=d, hs0=hs0):
                tile = i if d == 0 else n_tiles - 1 - i
                r = pl.multiple_of(tile * R, R)
                groups = range(R // 8) if d == 0 else range(R // 8 - 1, -1, -1)
                for g in groups:
                    rows = pl.ds(r + 8 * g, 8)
                    a = o_ref[0, rows, 0:LANE_BLK]
                    b = o_ref[0, rows, LANE_BLK:2 * LANE_BLK]
                    for s in (1, 2, 4):
                        shift = s if d == 0 else 8 - s
                        keep = (sub >= s) if d == 0 else (sub < 8 - s)
                        b = jnp.where(keep, a * pltpu.roll(b, shift, 0) + b, b)
                        a = jnp.where(keep, a * pltpu.roll(a, shift, 0), a)
                    h = a * carry + b
                    carry = h[7:8, :] if d == 0 else h[0:1, :]
                    if d == 0:
                        o_ref[0, rows, hs0:hs0 + LANE_BLK] = h
                    else:
                        o_ref[0, rows, hs0:hs0 + LANE_BLK] = o_ref[0, rows, hs0:hs0 + LANE_BLK] + h
                return carry
            lax.fori_loop(0, n_tiles, b_scan, jnp.zeros((1, LANE_BLK), F32))

    def y_proj(r):
        o_ref[0, pl.ds(r, RM), 0:D_LRU] = proj(r, COL_Y, D_LRU)
    _row_loop(L_TOT, RM, y_proj)

    def b_out(r):
        rows = pl.ds(r, R)
        y = o_ref[0, rows, 0:D_LRU]
        mix_ref[rows, D_CONV:D_CONV + D_LRU] = (
            jax.nn.gelu(y) * o_ref[0, rows, D_LRU:2 * D_LRU]).astype(BF16)
    _row_loop(L_TOT, R, b_out)

    def out_proj(r):
        rows = pl.ds(r, RM)
        m = jnp.dot(mix_ref[rows, :], w_out_ref[...], preferred_element_type=F32) + b_out_ref[...]
        o_ref[0, rows, :] = ALPHA * h_ref[0, rows, :] + m
    _row_loop(L_TOT, RM, out_proj)

    def out_norm(r):
        rows = pl.ds(r, R)
        o_ref[0, rows, :] = _layer_norm(o_ref[0, rows, :], lng_ref[...], lnb_ref[...])
    _row_loop(L_TOT, R, out_norm)


def _const_spec(block, index):
    return pl.BlockSpec(block, lambda i: index, pipeline_mode=pl.Buffered(1))


def _mixer(h, l, p):
    nb = h.shape[0]
    z2 = (l, 0, 0)
    in_specs = [
        pl.BlockSpec((1, L_TOT, D_MODEL), lambda i: (i, 0, 0)),
        _const_spec((None, D_MODEL, D_IN), z2),
        _const_spec((None, 1, D_IN), z2),
        _const_spec((None, CONV_A_WIDTH, D_CONV), z2),
        _const_spec((None, 1, D_CONV), z2),
        _const_spec((None, 1, D_CONV), z2),
        _const_spec((None, 1, D_CONV), z2),
        _const_spec((None, 2 * CONV_B_WIDTH, D_LRU), z2),
        _const_spec((None, 2, D_LRU), z2),
        _const_spec((None, 2, D_LRU // LANE_BLK, LANE_BLK, 2 * LANE_BLK), (l, 0, 0, 0, 0)),
        _const_spec((None, 2, D_LRU), z2),
        _const_spec((None, 2, D_LRU), z2),
        _const_spec((None, 2, D_LRU), z2),
        _const_spec((None, D_POOL, D_POOL), z2),
        _const_spec((None, 1, D_POOL), z2),
        _const_spec((None, 1, D_POOL), z2),
        _const_spec((None, D_MODEL, D_MODEL), z2),
        _const_spec((None, 1, D_MODEL), z2),
        _const_spec((None, 1, D_MODEL), z2),
        _const_spec((None, 1, D_MODEL), z2),
    ]
    return pl.pallas_call(
        _mixer_kernel,
        grid=(nb,),
        in_specs=in_specs,
        out_specs=pl.BlockSpec((1, L_TOT, D_MODEL), lambda i: (i, 0, 0)),
        out_shape=jax.ShapeDtypeStruct((nb, L_TOT, D_MODEL), F32),
        scratch_shapes=[
            pltpu.VMEM((L_TOT + 2 * HALO, D_LRU), F32),
            pltpu.VMEM((L_TOT, LANE_BLK), F32),
            pltpu.VMEM((L_TOT, D_MODEL), BF16),
        ],
        compiler_params=pltpu.CompilerParams(
            dimension_semantics=("arbitrary",), vmem_limit_bytes=VMEM_LIMIT),
        name=f"mixer{l}",
    )(h, p["w_in"], p["b_in"], p["conv_a_w"], p["conv_a_b"], p["norm_a_g"], p["norm_a_b"],
      p["conv_b_w"], p["conv_b_b"], p["w_gate"], p["gate_a_b"], p["gate_x_b"], p["lru_lambda"],
      p["pool_w"], p["pool_b"], p["pool_scale"], p["w_out"], p["b_out"], p["ln1_g"], p["ln1_b"])


def _ffn_kernel(h_ref, wg_ref, wv_ref, cfw_ref, cfb_ref, wd_ref, lng_ref, lnb_ref,
                o_ref, hb_ref, gp_ref, v_ref, p_ref):
    R = ROWS_EW
    RM = ROWS_MM
    j = pl.program_id(1)

    @pl.when(j == 0)
    def _init():
        gp_ref[0:8, :] = jnp.zeros((8, FF_BLK), F32)
        gp_ref[L_TOT + 8:L_TOT + 16, :] = jnp.zeros((8, FF_BLK), F32)

        def rows_init(r):
            rows = pl.ds(r, R)
            x = h_ref[0, rows, :]
            hb_ref[rows, :] = x.astype(BF16)
            o_ref[0, rows, :] = ALPHA * x
        _row_loop(L_TOT, R, rows_init)

    def up(r):
        lhs = hb_ref[pl.ds(r, RM), :]
        gp_ref[pl.ds(8 + r, RM), :] = jnp.dot(lhs, wg_ref[...], preferred_element_type=F32)
        v_ref[pl.ds(r, RM), :] = jnp.dot(lhs, wv_ref[...], preferred_element_type=F32)
    _row_loop(L_TOT, RM, up)

    def glu(r):
        win = gp_ref[pl.ds(r, R + 16), :]
        g = (cfw_ref[0:1, :] * win[7:7 + R] + cfw_ref[1:2, :] * win[8:8 + R]
             + cfw_ref[2:3, :] * win[9:9 + R] + cfb_ref[...])
        p_ref[pl.ds(r, R), :] = (jax.nn.gelu(g) * v_ref[pl.ds(r, R), :]).astype(BF16)
    _row_loop(L_TOT, R, glu)

    def down(r):
        rows = pl.ds(r, RM)
        o_ref[0, rows, :] = o_ref[0, rows, :] + jnp.dot(
            p_ref[rows, :], wd_ref[...], preferred_element_type=F32)
    _row_loop(L_TOT, RM, down)

    @pl.when(j == N_FF_BLK - 1)
    def _finish():
        def rows_norm(r):
            rows = pl.ds(r, R)
            o_ref[0, rows, :] = _layer_norm(o_ref[0, rows, :], lng_ref[...], lnb_ref[...])
        _row_loop(L_TOT, R, rows_norm)


def _ffn(h, l, p):
    nb = h.shape[0]
    in_specs = [
        pl.BlockSpec((1, L_TOT, D_MODEL), lambda i, j: (i, 0, 0)),
        pl.BlockSpec((None, D_MODEL, FF_BLK), lambda i, j: (l, 0, j)),
        pl.BlockSpec((None, D_MODEL, FF_BLK), lambda i, j: (l, 0, j + N_FF_BLK)),
        pl.BlockSpec((None, 3, FF_BLK), lambda i, j: (l, 0, j)),
        pl.BlockSpec((None, 1, FF_BLK), lambda i, j: (l, 0, j)),
        pl.BlockSpec((None, FF_BLK, D_MODEL), lambda i, j: (l, j, 0)),
        pl.BlockSpec((None, 1, D_MODEL), lambda i, j: (l, 0, 0)),
        pl.BlockSpec((None, 1, D_MODEL), lambda i, j: (l, 0, 0)),
    ]
    return pl.pallas_call(
        _ffn_kernel,
        grid=(nb, N_FF_BLK),
        in_specs=in_specs,
        out_specs=pl.BlockSpec((1, L_TOT, D_MODEL), lambda i, j: (i, 0, 0)),
        out_shape=jax.ShapeDtypeStruct((nb, L_TOT, D_MODEL), F32),
        scratch_shapes=[
            pltpu.VMEM((L_TOT, D_MODEL), BF16),
            pltpu.VMEM((L_TOT + 16, FF_BLK), F32),
            pltpu.VMEM((L_TOT, FF_BLK), F32),
            pltpu.VMEM((L_TOT, FF_BLK), BF16),
        ],
        compiler_params=pltpu.CompilerParams(
            dimension_semantics=("arbitrary", "arbitrary"), vmem_limit_bytes=VMEM_LIMIT),
        name=f"ffn{l}",
    )(h, p["w_up"], p["w_up"], p["conv_f_w"], p["conv_f_b"], p["w_down"], p["ln2_g"], p["ln2_b"])


def _block_diag(w, n):
    d = w.shape[-1]
    eye = jnp.eye(n, dtype=w.dtype)
    full = w[..., :, :, None, :] * eye[:, None, :, None]
    return full.reshape(w.shape[:-3] + (n * d, n * d))


def kernel(x, meta_tokens, emb_ln_g, emb_ln_b, w_in, b_in, conv_a_w, conv_a_b, norm_a_g, norm_a_b, conv_b_w, conv_b_b, gate_a_w, gate_a_b, gate_x_w, gate_x_b, lru_lambda, pool_w, pool_b, pool_scale, w_out, b_out, ln1_g, ln1_b, w_up, conv_f_w, conv_f_b, w_down, ln2_g, ln2_b):
    heads_per_blk = LANE_BLK // (D_LRU // N_LRU_HEADS)
    n_blk = D_LRU // LANE_BLK

    def gate_blocks(w):
        w = w.reshape(DEPTH, 2, n_blk, heads_per_blk, w.shape[-2], w.shape[-1])
        return _block_diag(w, heads_per_blk)

    row = lambda a: a.reshape(DEPTH, 1, a.shape[-1])
    p = {
        "w_in": w_in.astype(BF16),
        "b_in": row(b_in),
        "conv_a_w": conv_a_w,
        "conv_a_b": row(conv_a_b),
        "norm_a_g": row(norm_a_g),
        "norm_a_b": row(norm_a_b),
        "conv_b_w": conv_b_w.reshape(DEPTH, 2 * CONV_B_WIDTH, D_LRU),
        "conv_b_b": conv_b_b,
        "w_gate": jnp.concatenate([gate_blocks(gate_a_w), gate_blocks(gate_x_w)], axis=-1).astype(BF16),
        "gate_a_b": gate_a_b,
        "gate_x_b": gate_x_b,
        "lru_lambda": lru_lambda,
        "pool_w": _block_diag(pool_w, pool_w.shape[1]).astype(BF16),
        "pool_b": row(pool_b),
        "pool_scale": row(pool_scale),
        "w_out": w_out.astype(BF16),
        "b_out": row(b_out),
        "ln1_g": row(ln1_g),
        "ln1_b": row(ln1_b),
        "w_up": w_up.astype(BF16),
        "conv_f_w": conv_f_w,
        "conv_f_b": row(conv_f_b),
        "w_down": w_down.astype(BF16),
        "ln2_g": row(ln2_g),
        "ln2_b": row(ln2_b),
    }
    h = _embed(x, meta_tokens, emb_ln_g.reshape(1, D_MODEL), emb_ln_b.reshape(1, D_MODEL))
    for l in range(DEPTH):
        h = _mixer(h, l, p)
        h = _ffn(h, l, p)
    return h[:, N_META:]
```

```python
import math

import jax
import jax.numpy as jnp
from jax import lax
from jax.experimental import pallas as pl
from jax.experimental.pallas import tpu as pltpu

D_MODEL = 1024
SEQ = 2048
DEPTH = 4
N_META = 16
L_TOT = SEQ + N_META
D_CONV = 256
D_LRU = 512
D_POOL = 256
N_LRU_HEADS = 8
LRU_C = 8.0
CONV_A_WIDTH = 31
CONV_B_WIDTH = 4
POOL_GROUP_DIM = 64
D_FF = 2816
D_IN = 2 * D_CONV + 2 * D_LRU + D_POOL
ALPHA = (2.0 * DEPTH) ** 0.25
LN_EPS = 1e-5
GELU_C0 = math.sqrt(2.0 / math.pi)
GELU_C1 = GELU_C0 * 0.044715

COL_UG = 0
COL_Y = 2 * D_CONV
COL_X = 2 * D_CONV + D_LRU
COL_V = 2 * D_CONV + 2 * D_LRU

LANES = 128
ROWS_EW = 48
ROWS_MM = 688
ROWS_SUB = 16
HALO = 16
LANE_BLK = 256
FF_BLK = 256
N_FF_BLK = D_FF // FF_BLK
VMEM_LIMIT = 60 * 1024 * 1024

F32 = jnp.float32
BF16 = jnp.bfloat16


def _row_loop(n_rows, tile, fn):
    def body(i, c):
        fn(pl.multiple_of(i * tile, tile))
        return c
    lax.fori_loop(0, n_rows // tile, body, 0)


def _layer_norm(x, g, b):
    mu = jnp.mean(x, axis=-1, keepdims=True)
    xc = x - mu
    var = jnp.mean(xc * xc, axis=-1, keepdims=True)
    return xc * lax.rsqrt(var + LN_EPS) * g + b


def _layer_norm_1pass(x, g, b):
    inv_n = 1.0 / x.shape[-1]
    mu = jnp.sum(x, axis=-1, keepdims=True) * inv_n
    ex2 = jnp.sum(x * x, axis=-1, keepdims=True) * inv_n
    return (x - mu) * lax.rsqrt(ex2 - mu * mu + LN_EPS) * g + b


def _embed_kernel(x_ref, meta_ref, g_ref, b_ref, o_ref):
    g = g_ref[...]
    b = b_ref[...]
    o_ref[0, 0:N_META, :] = _layer_norm(meta_ref[...], g, b)

    def rows(r):
        o_ref[0, pl.ds(N_META + r, 64), :] = _layer_norm(x_ref[0, pl.ds(r, 64), :], g, b)
    _row_loop(SEQ, 64, rows)


def _embed(x, meta, g, b):
    nb = x.shape[0]
    return pl.pallas_call(
        _embed_kernel,
        grid=(nb,),
        in_specs=[
            pl.BlockSpec((1, SEQ, D_MODEL), lambda i: (i, 0, 0)),
            pl.BlockSpec((N_META, D_MODEL), lambda i: (0, 0)),
            pl.BlockSpec((1, D_MODEL), lambda i: (0, 0)),
            pl.BlockSpec((1, D_MODEL), lambda i: (0, 0)),
        ],
        out_specs=pl.BlockSpec((1, L_TOT, D_MODEL), lambda i: (i, 0, 0)),
        out_shape=jax.ShapeDtypeStruct((nb, L_TOT, D_MODEL), F32),
        compiler_params=pltpu.CompilerParams(
            dimension_semantics=("arbitrary",), vmem_limit_bytes=VMEM_LIMIT),
        name="embed",
    )(x, meta, g, b)


def _mixer_kernel(h_ref, w_in_ref, b_in_ref, caw_ref, cab_ref, nag_ref, nab_ref,
                  cbw_ref, cbb_ref, wg_ref, gab_ref, gxb_ref, lam_ref,
                  pw_ref, pb_ref, ps_ref, w_out_ref, b_out_ref, lng_ref, lnb_ref,
                  o_ref, xp_ref, s_ref, mix_ref):
    R = ROWS_EW
    RM = ROWS_MM

    xp_ref[0:HALO, :] = jnp.zeros((HALO, D_LRU), F32)
    xp_ref[L_TOT + HALO:L_TOT + 2 * HALO, :] = jnp.zeros((HALO, D_LRU), F32)

    def proj(r, c0, n):
        lhs = h_ref[0, pl.ds(r, RM), :].astype(BF16)
        return (jnp.dot(lhs, w_in_ref[:, c0:c0 + n], preferred_element_type=F32)
                + b_in_ref[:, c0:c0 + n])

    def a_proj(r):
        ug = proj(r, COL_UG, 2 * D_CONV)
        xp_ref[pl.ds(HALO + r, RM), 0:D_CONV] = ug[:, :D_CONV] * jax.nn.sigmoid(ug[:, D_CONV:])
    _row_loop(L_TOT, RM, a_proj)

    def a_conv(r):
        win = xp_ref[pl.ds(r, R + 2 * HALO), 0:D_CONV]
        acc = jnp.zeros((R, D_CONV), F32) + cab_ref[...]
        for s in range(8):
            shifted = win[s:s + R + 24]
            for q in range(4):
                k = 8 * q + s - 1
                if 0 <= k < CONV_A_WIDTH:
                    acc = acc + caw_ref[k:k + 1, :] * shifted[8 * q:8 * q + R]
        y = _layer_norm(acc, nag_ref[...], nab_ref[...])
        mix_ref[pl.ds(r, R), 0:D_CONV] = (y * jax.nn.sigmoid(y)).astype(BF16)
    _row_loop(L_TOT, R, a_conv)

    def c_proj(r):
        xp_ref[pl.ds(HALO + r, RM), 0:D_POOL] = proj(r, COL_V, D_POOL)
    _row_loop(L_TOT, RM, c_proj)

    lane = lax.broadcasted_iota(jnp.int32, (1, D_POOL), 1)
    half = jnp.where(lane < 64, 1, jnp.where(lane < 128, 2, jnp.where(lane < 192, 4, 8)))

    def c_pool(r):
        win = xp_ref[pl.ds(r, R + 2 * HALO), 0:D_POOL]
        s2 = win[:-1] + win[1:]
        s4 = s2[:-2] + s2[2:]
        s8 = s4[:-4] + s4[4:]
        s16 = s8[:-8] + s8[8:]
        pooled = jnp.where(lane < 64, s2[15:15 + R],
                           jnp.where(lane < 128, s4[14:14 + R],
                                     jnp.where(lane < 192, s8[12:12 + R], s16[8:8 + R])))
        t = r + lax.broadcasted_iota(jnp.int32, (R, D_POOL), 0)
        cnt = jnp.minimum(t + half, L_TOT) - jnp.maximum(t - half, 0)
        s_ref[pl.ds(r, R), :] = pooled / cnt.astype(F32) - win[HALO:HALO + R]
    _row_loop(L_TOT, R, c_pool)

    def c_out(r):
        m = s_ref[pl.ds(r, RM), :].astype(BF16)
        c = jnp.dot(m, pw_ref[...], preferred_element_type=F32) + pb_ref[...]
        mix_ref[pl.ds(r, RM), D_CONV + D_LRU:D_MODEL] = (c * ps_ref[...]).astype(BF16)
    _row_loop(L_TOT, RM, c_out)

    def b_proj(r):
        xp_ref[pl.ds(HALO + r, RM), :] = proj(r, COL_X, D_LRU)
    _row_loop(L_TOT, RM, b_proj)

    sub = lax.broadcasted_iota(jnp.int32, (8, LANE_BLK), 0)
    n_tiles = L_TOT // R

    for d in range(2):
        for cb in range(D_LRU // LANE_BLK):
            c0 = cb * LANE_BLK
            hs0 = D_LRU + c0

            def b_conv(r, d=d, c0=c0):
                win = xp_ref[pl.ds(r + 8, R + 16), c0:c0 + LANE_BLK]
                acc = jnp.zeros((R, LANE_BLK), F32) + cbb_ref[d:d + 1, c0:c0 + LANE_BLK]
                for k in range(CONV_B_WIDTH):
                    off = 8 + (k - (CONV_B_WIDTH - 1) if d == 0 else k)
                    w_k = cbw_ref[d * CONV_B_WIDTH + k:d * CONV_B_WIDTH + k + 1, c0:c0 + LANE_BLK]
                    acc = acc + w_k * win[off:off + R]
                s_ref[pl.ds(r, R), :] = acc
            _row_loop(L_TOT, R, b_conv)

            def b_gate_dot(r, d=d, cb=cb):
                xc = s_ref[pl.ds(r, RM), :].astype(BF16)
                o_ref[0, pl.ds(r, RM), 0:2 * LANE_BLK] = jnp.dot(
                    xc, wg_ref[d, cb], preferred_element_type=F32)
            _row_loop(L_TOT, RM, b_gate_dot)

            z = -lam_ref[d:d + 1, c0:c0 + LANE_BLK]
            softplus = jnp.maximum(z, 0.0) + jnp.log1p(jnp.exp(-jnp.abs(z)))
            decay = -LRU_C * softplus

            def b_gate(r, d=d, c0=c0, decay=decay):
                rows = pl.ds(r, R)
                rg = jax.nn.sigmoid(o_ref[0, rows, 0:LANE_BLK] + gab_ref[d:d + 1, c0:c0 + LANE_BLK])
                ig = jax.nn.sigmoid(o_ref[0, rows, LANE_BLK:2 * LANE_BLK]
                                    + gxb_ref[d:d + 1, c0:c0 + LANE_BLK])
                a = jnp.exp(rg * decay)
                o_ref[0, rows, 0:LANE_BLK] = a
                o_ref[0, rows, LANE_BLK:2 * LANE_BLK] = (
                    jnp.sqrt(1.0 - a * a) * (ig * s_ref[rows, :]))
            _row_loop(L_TOT, R, b_gate)

            def b_scan(i, carry, d=d, hs0=hs0):
                tile = i if d == 0 else n_tiles - 1 - i
                r = pl.multiple_of(tile * R, R)
                groups = range(R // 8) if d == 0 else range(R // 8 - 1, -1, -1)
                for g in groups:
                    rows = pl.ds(r + 8 * g, 8)
                    a = o_ref[0, rows, 0:LANE_BLK]
                    b = o_ref[0, rows, LANE_BLK:2 * LANE_BLK]
                    for s in (1, 2, 4):
                        shift = s if d == 0 else 8 - s
                        keep = (sub >= s) if d == 0 else (sub < 8 - s)
                        b = jnp.where(keep, a * pltpu.roll(b, shift, 0) + b, b)
                        a = jnp.where(keep, a * pltpu.roll(a, shift, 0), a)
                    h = a * carry + b
                    carry = h[7:8, :] if d == 0 else h[0:1, :]
                    if d == 0:
                        o_ref[0, rows, hs0:hs0 + LANE_BLK] = h
                    else:
                        o_ref[0, rows, hs0:hs0 + LANE_BLK] = o_ref[0, rows, hs0:hs0 + LANE_BLK] + h
                return carry
            lax.fori_loop(0, n_tiles, b_scan, jnp.zeros((1, LANE_BLK), F32))

    def y_proj(r):
        o_ref[0, pl.ds(r, RM), 0:D_LRU] = proj(r, COL_Y, D_LRU)
    _row_loop(L_TOT, RM, y_proj)

    def b_out(r):
        rows = pl.ds(r, R)
        y = o_ref[0, rows, 0:D_LRU]
        mix_ref[rows, D_CONV:D_CONV + D_LRU] = (
            jax.nn.gelu(y) * o_ref[0, rows, D_LRU:2 * D_LRU]).astype(BF16)
    _row_loop(L_TOT, R, b_out)

    def out_proj(r):
        rows = pl.ds(r, RM)
        m = jnp.dot(mix_ref[rows, :], w_out_ref[...], preferred_element_type=F32) + b_out_ref[...]
        o_ref[0, rows, :] = ALPHA * h_ref[0, rows, :] + m
    _row_loop(L_TOT, RM, out_proj)

    def out_norm(r):
        rows = pl.ds(r, R)
        o_ref[0, rows, :] = _layer_norm(o_ref[0, rows, :], lng_ref[...], lnb_ref[...])
    _row_loop(L_TOT, R, out_norm)


def _const_spec(block, index):
    return pl.BlockSpec(block, lambda i: index, pipeline_mode=pl.Buffered(1))


def _mixer(h, l, p):
    nb = h.shape[0]
    z2 = (l, 0, 0)
    in_specs = [
        pl.BlockSpec((1, L_TOT, D_MODEL), lambda i: (i, 0, 0)),
        _const_spec((None, D_MODEL, D_IN), z2),
        _const_spec((None, 1, D_IN), z2),
        _const_spec((None, CONV_A_WIDTH, D_CONV), z2),
        _const_spec((None, 1, D_CONV), z2),
        _const_spec((None, 1, D_CONV), z2),
        _const_spec((None, 1, D_CONV), z2),
        _const_spec((None, 2 * CONV_B_WIDTH, D_LRU), z2),
        _const_spec((None, 2, D_LRU), z2),
        _const_spec((None, 2, D_LRU // LANE_BLK, LANE_BLK, 2 * LANE_BLK), (l, 0, 0, 0, 0)),
        _const_spec((None, 2, D_LRU), z2),
        _const_spec((None, 2, D_LRU), z2),
        _const_spec((None, 2, D_LRU), z2),
        _const_spec((None, D_POOL, D_POOL), z2),
        _const_spec((None, 1, D_POOL), z2),
        _const_spec((None, 1, D_POOL), z2),
        _const_spec((None, D_MODEL, D_MODEL), z2),
        _const_spec((None, 1, D_MODEL), z2),
        _const_spec((None, 1, D_MODEL), z2),
        _const_spec((None, 1, D_MODEL), z2),
    ]
    return pl.pallas_call(
        _mixer_kernel,
        grid=(nb,),
        in_specs=in_specs,
        out_specs=pl.BlockSpec((1, L_TOT, D_MODEL), lambda i: (i, 0, 0)),
        out_shape=jax.ShapeDtypeStruct((nb, L_TOT, D_MODEL), F32),
        scratch_shapes=[
            pltpu.VMEM((L_TOT + 2 * HALO, D_LRU), F32),
            pltpu.VMEM((L_TOT, LANE_BLK), F32),
            pltpu.VMEM((L_TOT, D_MODEL), BF16),
        ],
        compiler_params=pltpu.CompilerParams(
            dimension_semantics=("arbitrary",), vmem_limit_bytes=VMEM_LIMIT),
        name=f"mixer{l}",
    )(h, p["w_in"], p["b_in"], p["conv_a_w"], p["conv_a_b"], p["norm_a_g"], p["norm_a_b"],
      p["conv_b_w"], p["conv_b_b"], p["w_gate"], p["gate_a_b"], p["gate_x_b"], p["lru_lambda"],
      p["pool_w"], p["pool_b"], p["pool_scale"], p["w_out"], p["b_out"], p["ln1_g"], p["ln1_b"])


def _ffn_kernel(h_ref, wg_ref, wv_ref, cfw_ref, cfb_ref, wd_ref, lng_ref, lnb_ref,
                o_ref, hb_ref, gp_ref, v_ref, p_ref):
    RM = ROWS_MM
    RS = ROWS_SUB
    n_lane_blk = FF_BLK // LANES
    j = pl.program_id(1)

    @pl.when(j == 0)
    def _init():
        for jb in range(n_lane_blk):
            gp_ref[jb, 0:8, :] = jnp.zeros((8, LANES), F32)
            gp_ref[jb, L_TOT + 8:L_TOT + 16, :] = jnp.zeros((8, LANES), F32)

        def rows_init(r):
            rows = pl.ds(r, ROWS_EW)
            x = h_ref[0, rows, :]
            hb_ref[rows, :] = x.astype(BF16)
            o_ref[0, rows, :] = ALPHA * x
        _row_loop(L_TOT, ROWS_EW, rows_init)

    def up(r):
        lhs = hb_ref[r:r + RM, :]
        g = jnp.dot(lhs, wg_ref[...], preferred_element_type=F32)
        for jb in range(n_lane_blk):
            gp_ref[jb, 8 + r:8 + r + RM, :] = g[:, jb * LANES:(jb + 1) * LANES]
        v_ref[r:r + RM, :] = jnp.dot(lhs, wv_ref[...], preferred_element_type=F32)

    def glu(r):
        for t0 in range(r, r + RM, RS):
            for jb in range(n_lane_blk):
                cols = slice(jb * LANES, (jb + 1) * LANES)
                g = (cfw_ref[0:1, cols] * gp_ref[jb, t0 + 7:t0 + 7 + RS, :]
                     + cfw_ref[1:2, cols] * gp_ref[jb, t0 + 8:t0 + 8 + RS, :]
                     + cfw_ref[2:3, cols] * gp_ref[jb, t0 + 9:t0 + 9 + RS, :]
                     + cfb_ref[:, cols])
                t = jnp.tanh(g * (GELU_C0 + GELU_C1 * (g * g)))
                p_ref[t0:t0 + RS, cols] = ((0.5 * g) * (1.0 + t) * v_ref[t0:t0 + RS, cols]).astype(BF16)

    def down(r):
        o_ref[0, r:r + RM, :] = o_ref[0, r:r + RM, :] + jnp.dot(
            p_ref[r:r + RM, :], wd_ref[...], preferred_element_type=F32)

    up(0)
    up(RM)
    glu(0)
    down(0)
    up(2 * RM)
    glu(RM)
    down(RM)
    glu(2 * RM)
    down(2 * RM)

    @pl.when(j == N_FF_BLK - 1)
    def _finish():
        for t0 in range(0, L_TOT, RS):
            o_ref[0, t0:t0 + RS, :] = _layer_norm_1pass(
                o_ref[0, t0:t0 + RS, :], lng_ref[...], lnb_ref[...])


def _ffn(h, l, p):
    nb = h.shape[0]
    in_specs = [
        pl.BlockSpec((1, L_TOT, D_MODEL), lambda i, j: (i, 0, 0)),
        pl.BlockSpec((None, D_MODEL, FF_BLK), lambda i, j: (l, 0, j)),
        pl.BlockSpec((None, D_MODEL, FF_BLK), lambda i, j: (l, 0, j + N_FF_BLK)),
        pl.BlockSpec((None, 3, FF_BLK), lambda i, j: (l, 0, j)),
        pl.BlockSpec((None, 1, FF_BLK), lambda i, j: (l, 0, j)),
        pl.BlockSpec((None, FF_BLK, D_MODEL), lambda i, j: (l, j, 0)),
        pl.BlockSpec((None, 1, D_MODEL), lambda i, j: (l, 0, 0)),
        pl.BlockSpec((None, 1, D_MODEL), lambda i, j: (l, 0, 0)),
    ]
    return pl.pallas_call(
        _ffn_kernel,
        grid=(nb, N_FF_BLK),
        in_specs=in_specs,
        out_specs=pl.BlockSpec((1, L_TOT, D_MODEL), lambda i, j: (i, 0, 0)),
        out_shape=jax.ShapeDtypeStruct((nb, L_TOT, D_MODEL), F32),
        scratch_shapes=[
            pltpu.VMEM((L_TOT, D_MODEL), BF16),
            pltpu.VMEM((FF_BLK // LANES, L_TOT + 16, LANES), F32),
            pltpu.VMEM((L_TOT, FF_BLK), F32),
            pltpu.VMEM((L_TOT, FF_BLK), BF16),
        ],
        compiler_params=pltpu.CompilerParams(
            dimension_semantics=("arbitrary", "arbitrary"), vmem_limit_bytes=VMEM_LIMIT),
        name=f"ffn{l}",
    )(h, p["w_up"], p["w_up"], p["conv_f_w"], p["conv_f_b"], p["w_down"], p["ln2_g"], p["ln2_b"])


def _block_diag(w, n):
    d = w.shape[-1]
    eye = jnp.eye(n, dtype=w.dtype)
    full = w[..., :, :, None, :] * eye[:, None, :, None]
    return full.reshape(w.shape[:-3] + (n * d, n * d))


def kernel(x, meta_tokens, emb_ln_g, emb_ln_b, w_in, b_in, conv_a_w, conv_a_b, norm_a_g, norm_a_b, conv_b_w, conv_b_b, gate_a_w, gate_a_b, gate_x_w, gate_x_b, lru_lambda, pool_w, pool_b, pool_scale, w_out, b_out, ln1_g, ln1_b, w_up, conv_f_w, conv_f_b, w_down, ln2_g, ln2_b):
    heads_per_blk = LANE_BLK // (D_LRU // N_LRU_HEADS)
    n_blk = D_LRU // LANE_BLK

    def gate_blocks(w):
        w = w.reshape(DEPTH, 2, n_blk, heads_per_blk, w.shape[-2], w.shape[-1])
        return _block_diag(w, heads_per_blk)

    row = lambda a: a.reshape(DEPTH, 1, a.shape[-1])
    p = {
        "w_in": w_in.astype(BF16),
        "b_in": row(b_in),
        "conv_a_w": conv_a_w,
        "conv_a_b": row(conv_a_b),
        "norm_a_g": row(norm_a_g),
        "norm_a_b": row(norm_a_b),
        "conv_b_w": conv_b_w.reshape(DEPTH, 2 * CONV_B_WIDTH, D_LRU),
        "conv_b_b": conv_b_b,
        "w_gate": jnp.concatenate([gate_blocks(gate_a_w), gate_blocks(gate_x_w)], axis=-1).astype(BF16),
        "gate_a_b": gate_a_b,
        "gate_x_b": gate_x_b,
        "lru_lambda": lru_lambda,
        "pool_w": _block_diag(pool_w, pool_w.shape[1]).astype(BF16),
        "pool_b": row(pool_b),
        "pool_scale": row(pool_scale),
        "w_out": w_out.astype(BF16),
        "b_out": row(b_out),
        "ln1_g": row(ln1_g),
        "ln1_b": row(ln1_b),
        "w_up": w_up.astype(BF16),
        "conv_f_w": conv_f_w,
        "conv_f_b": row(conv_f_b),
        "w_down": w_down.astype(BF16),
        "ln2_g": row(ln2_g),
        "ln2_b": row(ln2_b),
    }
    h = _embed(x, meta_tokens, emb_ln_g.reshape(1, D_MODEL), emb_ln_b.reshape(1, D_MODEL))
    for l in range(DEPTH):
        h = _mixer(h, l, p)
        h = _ffn(h, l, p)
    return h[:, N_META:]
```

```python
import math

import jax
import jax.numpy as jnp
from jax import lax
from jax.experimental import pallas as pl
from jax.experimental.pallas import tpu as pltpu

D_MODEL = 1024
SEQ = 2048
DEPTH = 4
N_META = 16
L_TOT = SEQ + N_META
D_CONV = 256
D_LRU = 512
D_POOL = 256
N_LRU_HEADS = 8
LRU_C = 8.0
CONV_A_WIDTH = 31
CONV_B_WIDTH = 4
D_FF = 2816
D_IN = 2 * D_CONV + 2 * D_LRU + D_POOL
ALPHA = (2.0 * DEPTH) ** 0.25
LN_EPS = 1e-5
GELU_C0 = math.sqrt(2.0 / math.pi)
GELU_C1 = GELU_C0 * 0.044715

COL_UG = 0
COL_Y = 2 * D_CONV
COL_X = 2 * D_CONV + D_LRU
COL_V = 2 * D_CONV + 2 * D_LRU

LANES = 128
SUBLANES = 8
ROWS_EW = 48
ROWS_MM = 688
ROWS_SUB = 16
HALO = 16
LANE_BLK = 256
SLABS = LANE_BLK // LANES
SCAN_CHUNK = L_TOT // SUBLANES
SCAN_UNROLL = 6
FF_BLK = 256
N_FF_BLK = D_FF // FF_BLK
VMEM_LIMIT = 60 * 1024 * 1024

F32 = jnp.float32
BF16 = jnp.bfloat16


def _row_loop(n_rows, tile, fn, first=0, last=None):
    last = n_rows // tile if last is None else last

    def body(i, c):
        fn(pl.multiple_of(i * tile, tile))
        return c
    lax.fori_loop(first, last, body, 0)


def _layer_norm(x, g, b):
    mu = jnp.mean(x, axis=-1, keepdims=True)
    xc = x - mu
    var = jnp.mean(xc * xc, axis=-1, keepdims=True)
    return xc * lax.rsqrt(var + LN_EPS) * g + b


def _layer_norm_1pass(x, g, b):
    inv_n = 1.0 / x.shape[-1]
    mu = jnp.sum(x, axis=-1, keepdims=True) * inv_n
    ex2 = jnp.sum(x * x, axis=-1, keepdims=True) * inv_n
    return (x - mu) * lax.rsqrt(ex2 - mu * mu + LN_EPS) * g + b


def _gelu_tanh(g):
    return (0.5 * g) * (1.0 + jnp.tanh(g * (GELU_C0 + GELU_C1 * (g * g))))


def _embed_kernel(x_ref, meta_ref, g_ref, b_ref, o_ref):
    g = g_ref[...]
    b = b_ref[...]
    o_ref[0, 0:N_META, :] = _layer_norm(meta_ref[...], g, b)

    def rows(r):
        o_ref[0, pl.ds(N_META + r, 64), :] = _layer_norm(x_ref[0, pl.ds(r, 64), :], g, b)
    _row_loop(SEQ, 64, rows)


def _embed(x, meta, g, b):
    nb = x.shape[0]
    return pl.pallas_call(
        _embed_kernel,
        grid=(nb,),
        in_specs=[
            pl.BlockSpec((1, SEQ, D_MODEL), lambda i: (i, 0, 0)),
            pl.BlockSpec((N_META, D_MODEL), lambda i: (0, 0)),
            pl.BlockSpec((1, D_MODEL), lambda i: (0, 0)),
            pl.BlockSpec((1, D_MODEL), lambda i: (0, 0)),
        ],
        out_specs=pl.BlockSpec((1, L_TOT, D_MODEL), lambda i: (i, 0, 0)),
        out_shape=jax.ShapeDtypeStruct((nb, L_TOT, D_MODEL), F32),
        compiler_params=pltpu.CompilerParams(
            dimension_semantics=("arbitrary",), vmem_limit_bytes=VMEM_LIMIT),
        name="embed",
    )(x, meta, g, b)


def _heads_kernel(h_ref, w_in_ref, b_in_ref, caw_ref, cab_ref, nag_ref, nab_ref,
                  cbw_ref, cbb_ref, wg_ref, gab_ref, gxb_ref, lam_ref,
                  pw_ref, pb_ref, ps_ref,
                  o_ref, xp_ref, xc_ref, gt_ref, ab_ref):
    R = ROWS_EW
    RM = ROWS_MM
    n_tiles = L_TOT // R

    for jb in range(xp_ref.shape[0]):
        xp_ref[jb, 0:HALO, :] = jnp.zeros((HALO, LANES), F32)
        xp_ref[jb, L_TOT + HALO:L_TOT + 2 * HALO, :] = jnp.zeros((HALO, LANES), F32)

    def proj(r, c0, n):
        lhs = h_ref[0, pl.ds(r, RM), :].astype(BF16)
        return (jnp.dot(lhs, w_in_ref[:, c0:c0 + n], preferred_element_type=F32)
                + b_in_ref[:, c0:c0 + n])

    def to_slabs(r, val):
        for jb in range(val.shape[1] // LANES):
            xp_ref[jb, pl.ds(HALO + r, RM), :] = val[:, jb * LANES:(jb + 1) * LANES]

    def a_proj(r):
        ug = proj(r, COL_UG, 2 * D_CONV)
        hu = 0.5 * ug[:, :D_CONV]
        to_slabs(r, hu + hu * jnp.tanh(0.5 * ug[:, D_CONV:]))
    _row_loop(L_TOT, RM, a_proj)

    def a_taps(r):
        accs = []
        for jb in range(D_CONV // LANES):
            cols = slice(jb * LANES, (jb + 1) * LANES)
            win = xp_ref.at[jb, pl.ds(r, R + 2 * HALO), :]
            acc = cab_ref[:, cols] + caw_ref[0:1, cols] * win[pl.ds(1, R), :]
            for k in range(1, CONV_A_WIDTH):
                acc = acc + caw_ref[k:k + 1, cols] * win[pl.ds(k + 1, R), :]
            accs.append(acc)
        return tuple(accs)

    def a_norm(r, accs):
        inv_n = 1.0 / D_CONV
        mu = sum(jnp.sum(a, axis=-1, keepdims=True) for a in accs) * inv_n
        ex2 = sum(jnp.sum(a * a, axis=-1, keepdims=True) for a in accs) * inv_n
        inv = lax.rsqrt(ex2 - mu * mu + LN_EPS)
        for jb, acc in enumerate(accs):
            cols = slice(jb * LANES, (jb + 1) * LANES)
            hy = 0.5 * ((acc - mu) * inv * nag_ref[:, cols] + nab_ref[:, cols])
            o_ref[0, pl.ds(r, R), cols] = (hy + hy * jnp.tanh(hy)).astype(BF16)

    def a_step(i, accs):
        nxt = a_taps(pl.multiple_of(i * R, R))
        a_norm(pl.multiple_of((i - 1) * R, R), accs)
        return nxt
    a_norm((n_tiles - 1) * R, lax.fori_loop(1, n_tiles, a_step, a_taps(0)))

    def c_proj(r):
        to_slabs(r, proj(r, COL_V, D_POOL))
    _row_loop(L_TOT, RM, c_proj)

    lane = lax.broadcasted_iota(jnp.int32, (1, LANES), 1)
    low = lane < 64

    def c_pool(r, edge):
        def taps(jb, lo, hi):
            win = xp_ref.at[jb, pl.ds(r, R + 2 * HALO), :]
            return [win[pl.ds(HALO + d, R), :] for d in range(lo, hi)]
        x0 = taps(0, -2, 2)
        s2 = x0[1] + x0[2]
        s4 = s2 + x0[0] + x0[3]
        x1 = taps(1, -8, 8)
        s8 = sum(x1[5:12], x1[4])
        s16 = sum(x1[0:4] + x1[12:16], s8)
        sums = (jnp.where(low, s2, s4), jnp.where(low, s8, s16))
        halves = ((1, 2), (4, 8))
        selfs = (x0[2], x1[8])
        for jb in range(2):
            if edge:
                t = r + lax.broadcasted_iota(jnp.int32, (R, LANES), 0)
                half = jnp.where(low, halves[jb][0], halves[jb][1])
                cnt = jnp.minimum(t + half, L_TOT) - jnp.maximum(t - half, 0)
                mean = sums[jb] / cnt.astype(F32)
            else:
                mean = sums[jb] * jnp.where(low, 0.5 / halves[jb][0], 0.5 / halves[jb][1])
            xc_ref[pl.ds(r, R), jb * LANES:(jb + 1) * LANES] = mean - selfs[jb]
    c_pool(0, True)
    _row_loop(L_TOT, R, lambda r: c_pool(r, False), first=1, last=n_tiles - 1)
    c_pool((n_tiles - 1) * R, True)

    def c_out(r):
        m = xc_ref[pl.ds(r, RM), :].astype(BF16)
        c = jnp.dot(m, pw_ref[...], preferred_element_type=F32) + pb_ref[...]
        o_ref[0, pl.ds(r, RM), D_CONV + D_LRU:D_MODEL] = (c * ps_ref[...]).astype(BF16)
    _row_loop(L_TOT, RM, c_out)

    sub = lax.broadcasted_iota(jnp.int32, (SUBLANES, LANES), 0)

    def slot(d, is_b, jb):
        return 4 * d + 2 * is_b + jb

    for cb in range(D_LRU // LANE_BLK):
        c0 = cb * LANE_BLK

        def x_proj(r, c0=c0):
            to_slabs(r, proj(r, COL_X + c0, LANE_BLK))
        _row_loop(L_TOT, RM, x_proj)

        for d in range(2):
            def b_conv(r, d=d, c0=c0):
                for jb in range(SLABS):
                    cols = slice(c0 + jb * LANES, c0 + (jb + 1) * LANES)
                    win = xp_ref.at[jb, pl.ds(r + 8, R + 16), :]
                    acc = cbb_ref[d:d + 1, cols]
                    for k in range(CONV_B_WIDTH):
                        off = 8 + (k - (CONV_B_WIDTH - 1) if d == 0 else k)
                        w_k = cbw_ref[d * CONV_B_WIDTH + k:d * CONV_B_WIDTH + k + 1, cols]
                        acc = acc + w_k * win[pl.ds(off, R), :]
                    xc_ref[pl.ds(r, R), jb * LANES:(jb + 1) * LANES] = acc
            _row_loop(L_TOT, R, b_conv)

            def b_gate_dot(r, d=d, cb=cb):
                xc = xc_ref[pl.ds(r, RM), :].astype(BF16)
                gt_ref[pl.ds(r, RM), :] = jnp.dot(xc, wg_ref[d, cb], preferred_element_type=F32)
            _row_loop(L_TOT, RM, b_gate_dot)

            z = -lam_ref[d:d + 1, c0:c0 + LANE_BLK]
            softplus = jnp.maximum(z, 0.0) + jnp.log1p(jnp.exp(-jnp.abs(z)))
            half_decay = (-0.5 * LRU_C) * softplus

            def b_gate(r, d=d, c0=c0, half_decay=half_decay):
                rows = pl.ds(r, R)
                for jb in range(SLABS):
                    lc = slice(jb * LANES, (jb + 1) * LANES)
                    gc = slice(c0 + jb * LANES, c0 + (jb + 1) * LANES)
                    tr = jnp.tanh(gt_ref[rows, lc] + gab_ref[d:d + 1, gc])
                    ti = jnp.tanh(gt_ref[rows, LANE_BLK + jb * LANES:LANE_BLK + (jb + 1) * LANES]
                                  + gxb_ref[d:d + 1, gc])
                    hd = half_decay[:, lc]
                    a = jnp.exp(hd + hd * tr)
                    ig = 0.5 + 0.5 * ti
                    ab_ref[slot(d, 0, jb), rows, :] = a
                    ab_ref[slot(d, 1, jb), rows, :] = jnp.sqrt(1.0 - a * a) * (ig * xc_ref[rows, lc])
            _row_loop(L_TOT, R, b_gate)

        chains = [(d, jb) for d in range(2) for jb in range(SLABS)]

        def chunk_rows(d, step, base=0):
            t2 = step if d == 0 else SCAN_CHUNK - 1 - step
            return pl.ds(base + t2, SUBLANES, stride=SCAN_CHUNK)

        def sweep_local(i, carry):
            hs, ps = list(carry[0]), list(carry[1])
            for u in range(SCAN_UNROLL):
                for c, (d, jb) in enumerate(chains):
                    rows = chunk_rows(d, i * SCAN_UNROLL + u)
                    a = ab_ref[slot(d, 0, jb), rows, :]
                    hs[c] = a * hs[c] + ab_ref[slot(d, 1, jb), rows, :]
                    ps[c] = a * ps[c]
            return tuple(hs), tuple(ps)
        zeros = tuple(jnp.zeros((SUBLANES, LANES), F32) for _ in chains)
        ones = tuple(jnp.ones((SUBLANES, LANES), F32) for _ in chains)
        h_fin, p_fin = lax.fori_loop(0, SCAN_CHUNK // SCAN_UNROLL, sweep_local, (zeros, ones))

        h_in = []
        for c, (d, jb) in enumerate(chains):
            first = (sub == 0) if d == 0 else (sub == SUBLANES - 1)
            shift = 1 if d == 0 else SUBLANES - 1
            init = jnp.zeros((SUBLANES, LANES), F32)
            for _ in range(SUBLANES - 1):
                true_fin = h_fin[c] + p_fin[c] * init
                init = jnp.where(first, 0.0, pltpu.roll(true_fin, shift, 0))
            h_in.append(init)

        def sweep_true(i, carry):
            hs = list(carry)
            for u in range(SCAN_UNROLL):
                for c, (d, jb) in enumerate(chains):
                    rows = chunk_rows(d, i * SCAN_UNROLL + u)
                    hs[c] = ab_ref[slot(d, 0, jb), rows, :] * hs[c] + ab_ref[slot(d, 1, jb), rows, :]
                    xp_ref[SLABS * d + jb, chunk_rows(d, i * SCAN_UNROLL + u, HALO), :] = hs[c]
            return tuple(hs)
        lax.fori_loop(0, SCAN_CHUNK // SCAN_UNROLL, sweep_true, tuple(h_in))

        def y_proj(r, c0=c0):
            gt_ref[pl.ds(r, RM), 0:LANE_BLK] = proj(r, COL_Y + c0, LANE_BLK)
        _row_loop(L_TOT, RM, y_proj)

        def b_out(r, c0=c0):
            rows = pl.ds(r, R)
            for jb in range(SLABS):
                y = gt_ref[rows, jb * LANES:(jb + 1) * LANES]
                oc = D_CONV + c0 + jb * LANES
                hrows = pl.ds(HALO + r, R)
                h_both = xp_ref[jb, hrows, :] + xp_ref[SLABS + jb, hrows, :]
                o_ref[0, rows, oc:oc + LANES] = (_gelu_tanh(y) * h_both).astype(BF16)
        _row_loop(L_TOT, R, b_out)


def _const_spec(block, index):
    return pl.BlockSpec(block, lambda *_: index, pipeline_mode=pl.Buffered(1))


def _heads(h, l, p):
    nb = h.shape[0]
    z2 = (l, 0, 0)
    in_specs = [
        pl.BlockSpec((1, L_TOT, D_MODEL), lambda i: (i, 0, 0)),
        _const_spec((None, D_MODEL, D_IN), z2),
        _const_spec((None, 1, D_IN), z2),
        _const_spec((None, CONV_A_WIDTH, D_CONV), z2),
        _const_spec((None, 1, D_CONV), z2),
        _const_spec((None, 1, D_CONV), z2),
        _const_spec((None, 1, D_CONV), z2),
        _const_spec((None, 2 * CONV_B_WIDTH, D_LRU), z2),
        _const_spec((None, 2, D_LRU), z2),
        _const_spec((None, 2, D_LRU // LANE_BLK, LANE_BLK, 2 * LANE_BLK), (l, 0, 0, 0, 0)),
        _const_spec((None, 2, D_LRU), z2),
        _const_spec((None, 2, D_LRU), z2),
        _const_spec((None, 2, D_LRU), z2),
        _const_spec((None, D_POOL, D_POOL), z2),
        _const_spec((None, 1, D_POOL), z2),
        _const_spec((None, 1, D_POOL), z2),
    ]
    return pl.pallas_call(
        _heads_kernel,
        grid=(nb,),
        in_specs=in_specs,
        out_specs=pl.BlockSpec((1, L_TOT, D_MODEL), lambda i: (i, 0, 0)),
        out_shape=jax.ShapeDtypeStruct((nb, L_TOT, D_MODEL), BF16),
        scratch_shapes=[
            pltpu.VMEM((2 * SLABS, L_TOT + 2 * HALO, LANES), F32),
            pltpu.VMEM((L_TOT, LANE_BLK), F32),
            pltpu.VMEM((L_TOT, 2 * LANE_BLK), F32),
            pltpu.VMEM((4 * SLABS, L_TOT, LANES), F32),
        ],
        compiler_params=pltpu.CompilerParams(
            dimension_semantics=("arbitrary",), vmem_limit_bytes=VMEM_LIMIT),
        name=f"heads{l}",
    )(h, p["w_in"], p["b_in"], p["conv_a_w"], p["conv_a_b"], p["norm_a_g"], p["norm_a_b"],
      p["conv_b_w"], p["conv_b_b"], p["w_gate"], p["gate_a_b"], p["gate_x_b"], p["lru_lambda"],
      p["pool_w"], p["pool_b"], p["pool_scale"])


def _outproj_kernel(h_ref, mix_ref, w_ref, b_ref, g_ref, beta_ref, o_ref):
    o_ref[0] = (ALPHA * h_ref[0]
                + jnp.dot(mix_ref[0], w_ref[...], preferred_element_type=F32) + b_ref[...])
    for t0 in range(0, ROWS_MM, ROWS_SUB):
        o_ref[0, t0:t0 + ROWS_SUB, :] = _layer_norm_1pass(
            o_ref[0, t0:t0 + ROWS_SUB, :], g_ref[...], beta_ref[...])


def _outproj(h, mix, l, p):
    nb = h.shape[0]
    tile = lambda i, t: (i, t, 0)
    return pl.pallas_call(
        _outproj_kernel,
        grid=(nb, L_TOT // ROWS_MM),
        in_specs=[
            pl.BlockSpec((1, ROWS_MM, D_MODEL), tile),
            pl.BlockSpec((1, ROWS_MM, D_MODEL), tile),
            _const_spec((None, D_MODEL, D_MODEL), (l, 0, 0)),
            _const_spec((None, 1, D_MODEL), (l, 0, 0)),
            _const_spec((None, 1, D_MODEL), (l, 0, 0)),
            _const_spec((None, 1, D_MODEL), (l, 0, 0)),
        ],
        out_specs=pl.BlockSpec((1, ROWS_MM, D_MODEL), tile),
        out_shape=jax.ShapeDtypeStruct((nb, L_TOT, D_MODEL), F32),
        compiler_params=pltpu.CompilerParams(
            dimension_semantics=("arbitrary", "arbitrary"), vmem_limit_bytes=VMEM_LIMIT),
        name=f"outproj{l}",
    )(h, mix, p["w_out"], p["b_out"], p["ln1_g"], p["ln1_b"])


def _ffn_kernel(h_ref, wg_ref, wv_ref, cfw_ref, cfb_ref, wd_ref, lng_ref, lnb_ref,
                o_ref, hb_ref, gp_ref, v_ref, p_ref):
    RM = ROWS_MM
    RS = ROWS_SUB
    n_lane_blk = FF_BLK // LANES
    j = pl.program_id(1)

    @pl.when(j == 0)
    def _init():
        for jb in range(n_lane_blk):
            gp_ref[jb, 0:8, :] = jnp.zeros((8, LANES), F32)
            gp_ref[jb, L_TOT + 8:L_TOT + 16, :] = jnp.zeros((8, LANES), F32)

        def rows_init(r):
            rows = pl.ds(r, ROWS_EW)
            x = h_ref[0, rows, :]
            hb_ref[rows, :] = x.astype(BF16)
            o_ref[0, rows, :] = ALPHA * x
        _row_loop(L_TOT, ROWS_EW, rows_init)

    def up(r):
        lhs = hb_ref[r:r + RM, :]
        g = jnp.dot(lhs, wg_ref[...], preferred_element_type=F32)
        for jb in range(n_lane_blk):
            gp_ref[jb, 8 + r:8 + r + RM, :] = g[:, jb * LANES:(jb + 1) * LANES]
        v_ref[r:r + RM, :] = jnp.dot(lhs, wv_ref[...], preferred_element_type=F32)

    def glu(r):
        for t0 in range(r, r + RM, RS):
            for jb in range(n_lane_blk):
                cols = slice(jb * LANES, (jb + 1) * LANES)
                g = (cfw_ref[0:1, cols] * gp_ref[jb, t0 + 7:t0 + 7 + RS, :]
                     + cfw_ref[1:2, cols] * gp_ref[jb, t0 + 8:t0 + 8 + RS, :]
                     + cfw_ref[2:3, cols] * gp_ref[jb, t0 + 9:t0 + 9 + RS, :]
                     + cfb_ref[:, cols])
                p_ref[t0:t0 + RS, cols] = (_gelu_tanh(g) * v_ref[t0:t0 + RS, cols]).astype(BF16)

    def down(r):
        o_ref[0, r:r + RM, :] = o_ref[0, r:r + RM, :] + jnp.dot(
            p_ref[r:r + RM, :], wd_ref[...], preferred_element_type=F32)

    up(0)
    up(RM)
    glu(0)
    down(0)
    up(2 * RM)
    glu(RM)
    down(RM)
    glu(2 * RM)
    down(2 * RM)

    @pl.when(j == N_FF_BLK - 1)
    def _finish():
        for t0 in range(0, L_TOT, RS):
            o_ref[0, t0:t0 + RS, :] = _layer_norm_1pass(
                o_ref[0, t0:t0 + RS, :], lng_ref[...], lnb_ref[...])


def _ffn(h, l, p):
    nb = h.shape[0]
    in_specs = [
        pl.BlockSpec((1, L_TOT, D_MODEL), lambda i, j: (i, 0, 0)),
        pl.BlockSpec((None, D_MODEL, FF_BLK), lambda i, j: (l, 0, j)),
        pl.BlockSpec((None, D_MODEL, FF_BLK), lambda i, j: (l, 0, j + N_FF_BLK)),
        pl.BlockSpec((None, 3, FF_BLK), lambda i, j: (l, 0, j)),
        pl.BlockSpec((None, 1, FF_BLK), lambda i, j: (l, 0, j)),
        pl.BlockSpec((None, FF_BLK, D_MODEL), lambda i, j: (l, j, 0)),
        pl.BlockSpec((None, 1, D_MODEL), lambda i, j: (l, 0, 0)),
        pl.BlockSpec((None, 1, D_MODEL), lambda i, j: (l, 0, 0)),
    ]
    return pl.pallas_call(
        _ffn_kernel,
        grid=(nb, N_FF_BLK),
        in_specs=in_specs,
        out_specs=pl.BlockSpec((1, L_TOT, D_MODEL), lambda i, j: (i, 0, 0)),
        out_shape=jax.ShapeDtypeStruct((nb, L_TOT, D_MODEL), F32),
        scratch_shapes=[
            pltpu.VMEM((L_TOT, D_MODEL), BF16),
            pltpu.VMEM((FF_BLK // LANES, L_TOT + 16, LANES), F32),
            pltpu.VMEM((L_TOT, FF_BLK), F32),
            pltpu.VMEM((L_TOT, FF_BLK), BF16),
        ],
        compiler_params=pltpu.CompilerParams(
            dimension_semantics=("arbitrary", "arbitrary"), vmem_limit_bytes=VMEM_LIMIT),
        name=f"ffn{l}",
    )(h, p["w_up"], p["w_up"], p["conv_f_w"], p["conv_f_b"], p["w_down"], p["ln2_g"], p["ln2_b"])


def _block_diag(w, n):
    d = w.shape[-1]
    eye = jnp.eye(n, dtype=w.dtype)
    full = w[..., :, :, None, :] * eye[:, None, :, None]
    return full.reshape(w.shape[:-3] + (n * d, n * d))


def _prepare_params(w_in, b_in, conv_a_w, conv_a_b, norm_a_g, norm_a_b, conv_b_w, conv_b_b,
                    gate_a_w, gate_a_b, gate_x_w, gate_x_b, lru_lambda, pool_w, pool_b, pool_scale,
                    w_out, b_out, ln1_g, ln1_b, w_up, conv_f_w, conv_f_b, w_down, ln2_g, ln2_b):
    depth = w_in.shape[0]
    heads_per_blk = LANE_BLK // (D_LRU // N_LRU_HEADS)
    n_blk = D_LRU // LANE_BLK

    def gate_blocks(w):
        w = w.reshape(depth, 2, n_blk, heads_per_blk, w.shape[-2], w.shape[-1])
        return _block_diag(w, heads_per_blk)

    row = lambda a: a.reshape(depth, 1, a.shape[-1])
    w_gate = 0.5 * jnp.concatenate([gate_blocks(gate_a_w), gate_blocks(gate_x_w)], axis=-1)
    return {
        "w_in": w_in.astype(BF16),
        "b_in": row(b_in),
        "conv_a_w": conv_a_w,
        "conv_a_b": row(conv_a_b),
        "norm_a_g": row(norm_a_g),
        "norm_a_b": row(norm_a_b),
        "conv_b_w": conv_b_w.reshape(depth, 2 * CONV_B_WIDTH, D_LRU),
        "conv_b_b": conv_b_b,
        "w_gate": w_gate.astype(BF16),
        "gate_a_b": 0.5 * gate_a_b,
        "gate_x_b": 0.5 * gate_x_b,
        "lru_lambda": lru_lambda,
        "pool_w": _block_diag(pool_w, pool_w.shape[1]).astype(BF16),
        "pool_b": row(pool_b),
        "pool_scale": row(pool_scale),
        "w_out": w_out.astype(BF16),
        "b_out": row(b_out),
        "ln1_g": row(ln1_g),
        "ln1_b": row(ln1_b),
        "w_up": w_up.astype(BF16),
        "conv_f_w": conv_f_w,
        "conv_f_b": row(conv_f_b),
        "w_down": w_down.astype(BF16),
        "ln2_g": row(ln2_g),
        "ln2_b": row(ln2_b),
    }


def kernel(x, meta_tokens, emb_ln_g, emb_ln_b, w_in, b_in, conv_a_w, conv_a_b, norm_a_g, norm_a_b, conv_b_w, conv_b_b, gate_a_w, gate_a_b, gate_x_w, gate_x_b, lru_lambda, pool_w, pool_b, pool_scale, w_out, b_out, ln1_g, ln1_b, w_up, conv_f_w, conv_f_b, w_down, ln2_g, ln2_b):
    p = _prepare_params(w_in, b_in, conv_a_w, conv_a_b, norm_a_g, norm_a_b, conv_b_w, conv_b_b,
                        gate_a_w, gate_a_b, gate_x_w, gate_x_b, lru_lambda, pool_w, pool_b,
                        pool_scale, w_out, b_out, ln1_g, ln1_b, w_up, conv_f_w, conv_f_b, w_down,
                        ln2_g, ln2_b)
    h = _embed(x, meta_tokens, emb_ln_g.reshape(1, D_MODEL), emb_ln_b.reshape(1, D_MODEL))
    for l in range(DEPTH):
        mix = _heads(h, l, p)
        h = _outproj(h, mix, l, p)
        h = _ffn(h, l, p)
    return h[:, N_META:]
```

```python
import functools
import math

import jax
import jax.numpy as jnp
from jax import lax
from jax.experimental import pallas as pl
from jax.experimental.pallas import tpu as pltpu

D_MODEL = 1024
SEQ = 2048
DEPTH = 4
N_META = 16
L_TOT = SEQ + N_META
D_CONV = 256
D_LRU = 512
D_POOL = 256
N_LRU_HEADS = 8
LRU_C = 8.0
CONV_A_WIDTH = 31
CONV_B_WIDTH = 4
D_FF = 2816
D_IN = 2 * D_CONV + 2 * D_LRU + D_POOL
ALPHA = (2.0 * DEPTH) ** 0.25
LN_EPS = 1e-5
GELU_C0 = math.sqrt(2.0 / math.pi)
GELU_C1 = GELU_C0 * 0.044715
LOG2_E = 1.0 / math.log(2.0)
SQRT_GUARD = 1e-30

COL_UG = 0
COL_Y = 2 * D_CONV
COL_X = 2 * D_CONV + D_LRU
COL_V = 2 * D_CONV + 2 * D_LRU

LANES = 128
SUBLANES = 8
ROWS_EW = 48
ROWS_MM = 688
ROWS_SUB = 16
OUTPROJ_SPLIT = 22 * ROWS_SUB
HALO = 16
LANE_BLK = 256
SLABS = LANE_BLK // LANES
SCAN_CHUNK = L_TOT // SUBLANES
SCAN_UNROLL = 6
FF_BLK = 256
N_FF_BLK = D_FF // FF_BLK
VMEM_LIMIT = 60 * 1024 * 1024
HEADS_VMEM_LIMIT = 62 * 1024 * 1024

F32 = jnp.float32
BF16 = jnp.bfloat16


def _row_loop(n_rows, tile, fn, first=0, last=None):
    last = n_rows // tile if last is None else last

    def body(i, c):
        fn(pl.multiple_of(i * tile, tile))
        return c
    lax.fori_loop(first, last, body, 0)


def _layer_norm(x, g, b):
    mu = jnp.mean(x, axis=-1, keepdims=True)
    xc = x - mu
    var = jnp.mean(xc * xc, axis=-1, keepdims=True)
    return xc * lax.rsqrt(var + LN_EPS) * g + b


def _layer_norm_1pass(x, g, b):
    inv_n = 1.0 / x.shape[-1]
    mu = jnp.sum(x, axis=-1, keepdims=True) * inv_n
    ex2 = jnp.sum(x * x, axis=-1, keepdims=True) * inv_n
    return (x - mu) * lax.rsqrt(ex2 - mu * mu + LN_EPS) * g + b


def _gelu_tanh(g):
    return (0.5 * g) * (1.0 + jnp.tanh(g * (GELU_C0 + GELU_C1 * (g * g))))


def _embed_kernel(x_ref, meta_ref, g_ref, b_ref, o_ref):
    g = g_ref[...]
    b = b_ref[...]
    o_ref[0, 0:N_META, :] = _layer_norm(meta_ref[...], g, b)

    def rows(r):
        o_ref[0, pl.ds(N_META + r, 64), :] = _layer_norm(x_ref[0, pl.ds(r, 64), :], g, b)
    _row_loop(SEQ, 64, rows)


def _embed(x, meta, g, b):
    nb = x.shape[0]
    return pl.pallas_call(
        _embed_kernel,
        grid=(nb,),
        in_specs=[
            pl.BlockSpec((1, SEQ, D_MODEL), lambda i: (i, 0, 0)),
            pl.BlockSpec((N_META, D_MODEL), lambda i: (0, 0)),
            pl.BlockSpec((1, D_MODEL), lambda i: (0, 0)),
            pl.BlockSpec((1, D_MODEL), lambda i: (0, 0)),
        ],
        out_specs=pl.BlockSpec((1, L_TOT, D_MODEL), lambda i: (i, 0, 0)),
        out_shape=jax.ShapeDtypeStruct((nb, L_TOT, D_MODEL), F32),
        compiler_params=pltpu.CompilerParams(
            dimension_semantics=("arbitrary",), vmem_limit_bytes=VMEM_LIMIT),
        name="embed",
    )(x, meta, g, b)


def _heads_kernel(h_ref, w_in_ref, b_in_ref, caw_ref, cab_ref, nag_ref, nab_ref,
                  cbw_ref, cbb_ref, wg_ref, gb_ref, lam_ref,
                  pw_ref, pb_ref, ps_ref,
                  o_ref, xa_ref, xv_ref, xx_ref, xc_ref, gt_ref, ab_ref, gy_ref, lhs_ref):
    R = ROWS_EW
    RM = ROWS_MM
    RS = ROWS_SUB
    n_tiles = L_TOT // R

    for ref in (xa_ref, xv_ref, xx_ref):
        for jb in range(ref.shape[0]):
            ref[jb, 0:HALO, :] = jnp.zeros((HALO, LANES), F32)
            ref[jb, L_TOT + HALO:L_TOT + 2 * HALO, :] = jnp.zeros((HALO, LANES), F32)

    def proj(c0, n):
        return (jnp.dot(lhs_ref[...], w_in_ref[:, c0:c0 + n], preferred_element_type=F32)
                + b_in_ref[:, c0:c0 + n])

    def to_slabs(ref, r, val):
        for jb in range(val.shape[1] // LANES):
            ref[jb, pl.ds(HALO + r, RM), :] = val[:, jb * LANES:(jb + 1) * LANES]

    def a_proj(r):
        lhs_ref[...] = h_ref[0, pl.ds(r, RM), :].astype(BF16)
        ug = proj(COL_UG, 2 * D_CONV)
        hu = 0.5 * ug[:, :D_CONV]
        to_slabs(xa_ref, r, hu + hu * jnp.tanh(0.5 * ug[:, D_CONV:]))
    _row_loop(L_TOT, RM, a_proj)

    def a_conv_and_proj(r):
        lhs_ref[...] = h_ref[0, pl.ds(r, RM), :].astype(BF16)
        to_slabs(xx_ref, r, proj(COL_X, D_LRU))
        gt_ref[pl.ds(r, RM), :] = proj(COL_Y, D_LRU)
        to_slabs(xv_ref, r, proj(COL_V, D_POOL))

        wins = [xa_ref.at[jb, pl.ds(r, RM + 2 * HALO), :] for jb in range(D_CONV // LANES)]
        for t0 in range(0, RM, RS):
            rows = pl.ds(r + t0, RS)
            gy_ref[rows, :] = _gelu_tanh(gt_ref[rows, :]).astype(BF16)
            accs = []
            for jb, win in enumerate(wins):
                cols = slice(jb * LANES, (jb + 1) * LANES)
                acc = cab_ref[:, cols] + caw_ref[0:1, cols] * win[pl.ds(t0 + 1, RS), :]
                for k in range(1, CONV_A_WIDTH):
                    acc = acc + caw_ref[k:k + 1, cols] * win[pl.ds(t0 + k + 1, RS), :]
                accs.append(acc)
            inv_n = 1.0 / D_CONV
            mu = sum(jnp.sum(a, axis=-1, keepdims=True) for a in accs) * inv_n
            ex2 = sum(jnp.sum(a * a, axis=-1, keepdims=True) for a in accs) * inv_n
            inv = lax.rsqrt(ex2 - mu * mu + LN_EPS)
            for jb, acc in enumerate(accs):
                cols = slice(jb * LANES, (jb + 1) * LANES)
                hy = 0.5 * ((acc - mu) * inv * nag_ref[:, cols] + nab_ref[:, cols])
                o_ref[0, pl.ds(r + t0, RS), cols] = (hy + hy * jnp.tanh(hy)).astype(BF16)
    _row_loop(L_TOT, RM, a_conv_and_proj)

    lane = lax.broadcasted_iota(jnp.int32, (1, LANES), 1)
    low = lane < 64

    def c_pool(r, edge):
        def taps(jb, lo, hi):
            win = xv_ref.at[jb, pl.ds(r, R + 2 * HALO), :]
            return [win[pl.ds(HALO + d, R), :] for d in range(lo, hi)]
        x0 = taps(0, -2, 2)
        s2 = x0[1] + x0[2]
        s4 = s2 + x0[0] + x0[3]
        x1 = taps(1, -8, 8)
        s8 = sum(x1[5:12], x1[4])
        s16 = sum(x1[0:4] + x1[12:16], s8)
        sums = (jnp.where(low, s2, s4), jnp.where(low, s8, s16))
        halves = ((1, 2), (4, 8))
        selfs = (x0[2], x1[8])
        for jb in range(2):
            if edge:
                t = r + lax.broadcasted_iota(jnp.int32, (R, LANES), 0)
                half = jnp.where(low, halves[jb][0], halves[jb][1])
                cnt = jnp.minimum(t + half, L_TOT) - jnp.maximum(t - half, 0)
                mean = sums[jb] / cnt.astype(F32)
            else:
                mean = sums[jb] * jnp.where(low, 0.5 / halves[jb][0], 0.5 / halves[jb][1])
            xc_ref[pl.ds(r, R), jb * LANES:(jb + 1) * LANES] = mean - selfs[jb]
    c_pool(0, True)
    _row_loop(L_TOT, R, lambda r: c_pool(r, False), first=1, last=n_tiles - 1)
    c_pool((n_tiles - 1) * R, True)

    def c_out(r):
        m = xc_ref[pl.ds(r, RM), :].astype(BF16)
        c = jnp.dot(m, pw_ref[...], preferred_element_type=F32) + pb_ref[...]
        o_ref[0, pl.ds(r, RM), D_CONV + D_LRU:D_MODEL] = (c * ps_ref[...]).astype(BF16)
    _row_loop(L_TOT, RM, c_out)

    sub = lax.broadcasted_iota(jnp.int32, (SUBLANES, LANES), 0)

    def slot(d, is_b, jb):
        return 4 * d + 2 * is_b + jb

    for cb in range(D_LRU // LANE_BLK):
        c0 = cb * LANE_BLK

        for d in range(2):
            z = -lam_ref[d:d + 1, c0:c0 + LANE_BLK]
            softplus = jnp.maximum(z, 0.0) + jnp.log1p(jnp.exp(-jnp.abs(z)))
            half_decay_log2 = (-0.5 * LRU_C * LOG2_E) * softplus

            for r in range(0, L_TOT, RM):
                for t0 in range(r, r + RM, RS):
                    for jb in range(SLABS):
                        cols = slice(c0 + jb * LANES, c0 + (jb + 1) * LANES)
                        acc = cbb_ref[d:d + 1, cols]
                        for k in range(CONV_B_WIDTH):
                            off = HALO + t0 + (k - (CONV_B_WIDTH - 1) if d == 0 else k)
                            w_k = cbw_ref[d * CONV_B_WIDTH + k:d * CONV_B_WIDTH + k + 1, cols]
                            acc = acc + w_k * xx_ref[SLABS * cb + jb, off:off + RS, :]
                        xc_ref[t0:t0 + RS, jb * LANES:(jb + 1) * LANES] = acc

                gt_ref[r:r + RM, :] = (
                    jnp.dot(xc_ref[r:r + RM, :].astype(BF16), wg_ref[d, cb], preferred_element_type=F32)
                    + gb_ref[d, cb:cb + 1, :])

                for t0 in range(r, r + RM, RS):
                    rows = slice(t0, t0 + RS)
                    for jb in range(SLABS):
                        lc = slice(jb * LANES, (jb + 1) * LANES)
                        tr = jnp.tanh(gt_ref[rows, lc])
                        ti = jnp.tanh(gt_ref[rows, LANE_BLK + jb * LANES:LANE_BLK + (jb + 1) * LANES])
                        hd = half_decay_log2[:, lc]
                        a = jnp.exp2(hd + hd * tr)
                        ab_ref[slot(d, 0, jb), rows, :] = a
                        s = 1.0 - a * a
                        ab_ref[slot(d, 1, jb), rows, :] = (
                            (s * lax.rsqrt(jnp.maximum(s, SQRT_GUARD)))
                            * ((0.5 + 0.5 * ti) * xc_ref[rows, lc]))

        chains = [(d, jb) for d in range(2) for jb in range(SLABS)]
        h_out = (xa_ref, xv_ref)

        def chunk_rows(d, step, base=0):
            t2 = step if d == 0 else SCAN_CHUNK - 1 - step
            return pl.ds(base + t2, SUBLANES, stride=SCAN_CHUNK)

        def sweep_local(i, carry):
            hs, ps = list(carry[0]), list(carry[1])
            for u in range(SCAN_UNROLL):
                for c, (d, jb) in enumerate(chains):
                    rows = chunk_rows(d, i * SCAN_UNROLL + u)
                    a = ab_ref[slot(d, 0, jb), rows, :]
                    hs[c] = a * hs[c] + ab_ref[slot(d, 1, jb), rows, :]
                    ps[c] = a * ps[c]
            return tuple(hs), tuple(ps)
        zeros = tuple(jnp.zeros((SUBLANES, LANES), F32) for _ in chains)
        ones = tuple(jnp.ones((SUBLANES, LANES), F32) for _ in chains)
        h_fin, p_fin = lax.fori_loop(0, SCAN_CHUNK // SCAN_UNROLL, sweep_local, (zeros, ones))

        h_in = []
        for c, (d, jb) in enumerate(chains):
            first = (sub == 0) if d == 0 else (sub == SUBLANES - 1)
            shift = 1 if d == 0 else SUBLANES - 1
            init = jnp.zeros((SUBLANES, LANES), F32)
            for _ in range(SUBLANES - 1):
                true_fin = h_fin[c] + p_fin[c] * init
                init = jnp.where(first, 0.0, pltpu.roll(true_fin, shift, 0))
            h_in.append(init)

        def sweep_true(i, carry):
            hs = list(carry)
            for u in range(SCAN_UNROLL):
                for c, (d, jb) in enumerate(chains):
                    rows = chunk_rows(d, i * SCAN_UNROLL + u)
                    hs[c] = ab_ref[slot(d, 0, jb), rows, :] * hs[c] + ab_ref[slot(d, 1, jb), rows, :]
                    h_out[d][jb, chunk_rows(d, i * SCAN_UNROLL + u, HALO), :] = hs[c]
            return tuple(hs)
        lax.fori_loop(0, SCAN_CHUNK // SCAN_UNROLL, sweep_true, tuple(h_in))

        def b_out(r, c0=c0):
            rows = pl.ds(r, R)
            hrows = pl.ds(HALO + r, R)
            for jb in range(SLABS):
                gy = gy_ref[rows, c0 + jb * LANES:c0 + (jb + 1) * LANES].astype(F32)
                oc = D_CONV + c0 + jb * LANES
                o_ref[0, rows, oc:oc + LANES] = (
                    gy * (xa_ref[jb, hrows, :] + xv_ref[jb, hrows, :])).astype(BF16)
        _row_loop(L_TOT, R, b_out)


def _const_spec(block, index):
    return pl.BlockSpec(block, lambda *_: index, pipeline_mode=pl.Buffered(1))


def _heads(h, l, p):
    nb = h.shape[0]
    z2 = (l, 0, 0)
    in_specs = [
        pl.BlockSpec((1, L_TOT, D_MODEL), lambda i: (i, 0, 0)),
        _const_spec((None, D_MODEL, D_IN), z2),
        _const_spec((None, 1, D_IN), z2),
        _const_spec((None, CONV_A_WIDTH, D_CONV), z2),
        _const_spec((None, 1, D_CONV), z2),
        _const_spec((None, 1, D_CONV), z2),
        _const_spec((None, 1, D_CONV), z2),
        _const_spec((None, 2 * CONV_B_WIDTH, D_LRU), z2),
        _const_spec((None, 2, D_LRU), z2),
        _const_spec((None, 2, D_LRU // LANE_BLK, LANE_BLK, 2 * LANE_BLK), (l, 0, 0, 0, 0)),
        _const_spec((None, 2, D_LRU // LANE_BLK, 2 * LANE_BLK), (l, 0, 0, 0)),
        _const_spec((None, 2, D_LRU), z2),
        _const_spec((None, D_POOL, D_POOL), z2),
        _const_spec((None, 1, D_POOL), z2),
        _const_spec((None, 1, D_POOL), z2),
    ]
    return pl.pallas_call(
        _heads_kernel,
        grid=(nb,),
        in_specs=in_specs,
        out_specs=pl.BlockSpec((1, L_TOT, D_MODEL), lambda i: (i, 0, 0)),
        out_shape=jax.ShapeDtypeStruct((nb, L_TOT, D_MODEL), BF16),
        scratch_shapes=[
            pltpu.VMEM((SLABS, L_TOT + 2 * HALO, LANES), F32),
            pltpu.VMEM((SLABS, L_TOT + 2 * HALO, LANES), F32),
            pltpu.VMEM((2 * SLABS, L_TOT + 2 * HALO, LANES), F32),
            pltpu.VMEM((L_TOT, LANE_BLK), F32),
            pltpu.VMEM((L_TOT, 2 * LANE_BLK), F32),
            pltpu.VMEM((4 * SLABS, L_TOT, LANES), F32),
            pltpu.VMEM((L_TOT, D_LRU), BF16),
            pltpu.VMEM((ROWS_MM, D_MODEL), BF16),
        ],
        compiler_params=pltpu.CompilerParams(
            dimension_semantics=("arbitrary",), vmem_limit_bytes=HEADS_VMEM_LIMIT),
        name=f"heads{l}",
    )(h, p["w_in"], p["b_in"], p["conv_a_w"], p["conv_a_b"], p["norm_a_g"], p["norm_a_b"],
      p["conv_b_w"], p["conv_b_b"], p["w_gate"], p["gate_b"], p["lru_lambda"],
      p["pool_w"], p["pool_b"], p["pool_scale"])


def _outproj_kernel(h_ref, mix_ref, w_ref, b_ref, g_ref, beta_ref, o_ref):
    for lo, hi in ((0, OUTPROJ_SPLIT), (OUTPROJ_SPLIT, ROWS_MM)):
        o_ref[0, lo:hi, :] = (
            ALPHA * h_ref[0, lo:hi, :]
            + jnp.dot(mix_ref[0, lo:hi, :], w_ref[...], preferred_element_type=F32) + b_ref[...])
        for t0 in range(lo, hi, ROWS_SUB):
            o_ref[0, t0:t0 + ROWS_SUB, :] = _layer_norm_1pass(
                o_ref[0, t0:t0 + ROWS_SUB, :], g_ref[...], beta_ref[...])


def _outproj(h, mix, l, p):
    nb = h.shape[0]
    tile = lambda i, t: (i, t, 0)
    return pl.pallas_call(
        _outproj_kernel,
        grid=(nb, L_TOT // ROWS_MM),
        in_specs=[
            pl.BlockSpec((1, ROWS_MM, D_MODEL), tile),
            pl.BlockSpec((1, ROWS_MM, D_MODEL), tile),
            _const_spec((None, D_MODEL, D_MODEL), (l, 0, 0)),
            _const_spec((None, 1, D_MODEL), (l, 0, 0)),
            _const_spec((None, 1, D_MODEL), (l, 0, 0)),
            _const_spec((None, 1, D_MODEL), (l, 0, 0)),
        ],
        out_specs=pl.BlockSpec((1, ROWS_MM, D_MODEL), tile),
        out_shape=jax.ShapeDtypeStruct((nb, L_TOT, D_MODEL), F32),
        compiler_params=pltpu.CompilerParams(
            dimension_semantics=("arbitrary", "arbitrary"), vmem_limit_bytes=VMEM_LIMIT),
        name=f"outproj{l}",
    )(h, mix, p["w_out"], p["b_out"], p["ln1_g"], p["ln1_b"])


def _ffn_kernel(h_ref, wg_ref, wv_ref, cfw_ref, cfb_ref, wd_ref, lng_ref, lnb_ref,
                o_ref, acc_ref, hb_ref, gp_ref, v_ref, p_ref, *, drop_rows):
    RM = ROWS_MM
    RS = ROWS_SUB
    n_lane_blk = FF_BLK // LANES
    j = pl.program_id(1)

    @pl.when(j == 0)
    def _init():
        for jb in range(n_lane_blk):
            gp_ref[jb, 0:8, :] = jnp.zeros((8, LANES), F32)
            gp_ref[jb, L_TOT + 8:L_TOT + 16, :] = jnp.zeros((8, LANES), F32)

        def rows_init(r):
            rows = pl.ds(r, ROWS_EW)
            x = h_ref[0, rows, :]
            hb_ref[rows, :] = x.astype(BF16)
            acc_ref[rows, :] = ALPHA * x
        _row_loop(L_TOT, ROWS_EW, rows_init)

    def up(r):
        g = jnp.dot(hb_ref[r:r + RM, :], wg_ref[...], preferred_element_type=F32)
        for jb in range(n_lane_blk):
            gp_ref[jb, 8 + r:8 + r + RM, :] = g[:, jb * LANES:(jb + 1) * LANES]
        v_ref[r:r + RM, :] = jnp.dot(hb_ref[r:r + RM, :], wv_ref[...], preferred_element_type=F32)

    def glu(r):
        for t0 in range(r, r + RM, RS):
            for jb in range(n_lane_blk):
                cols = slice(jb * LANES, (jb + 1) * LANES)
                g = (cfw_ref[0:1, cols] * gp_ref[jb, t0 + 7:t0 + 7 + RS, :]
                     + cfw_ref[1:2, cols] * gp_ref[jb, t0 + 8:t0 + 8 + RS, :]
                     + cfw_ref[2:3, cols] * gp_ref[jb, t0 + 9:t0 + 9 + RS, :]
                     + cfb_ref[:, cols])
                p_ref[t0:t0 + RS, cols] = (_gelu_tanh(g) * v_ref[t0:t0 + RS, cols]).astype(BF16)

    def down(r):
        acc_ref[r:r + RM, :] = acc_ref[r:r + RM, :] + jnp.dot(
            p_ref[r:r + RM, :], wd_ref[...], preferred_element_type=F32)

    up(0)
    up(RM)
    glu(0)
    down(0)
    up(2 * RM)
    glu(RM)
    down(RM)
    glu(2 * RM)
    down(2 * RM)

    @pl.when(j == N_FF_BLK - 1)
    def _finish():
        for t0 in range(drop_rows, L_TOT, RS):
            o_ref[0, t0 - drop_rows:t0 - drop_rows + RS, :] = _layer_norm_1pass(
                acc_ref[t0:t0 + RS, :], lng_ref[...], lnb_ref[...])


def _ffn(h, l, p, drop_rows):
    nb = h.shape[0]
    out_rows = L_TOT - drop_rows
    in_specs = [
        pl.BlockSpec((1, L_TOT, D_MODEL), lambda i, j: (i, 0, 0)),
        pl.BlockSpec((None, D_MODEL, FF_BLK), lambda i, j: (l, 0, j)),
        pl.BlockSpec((None, D_MODEL, FF_BLK), lambda i, j: (l, 0, j + N_FF_BLK)),
        pl.BlockSpec((None, 3, FF_BLK), lambda i, j: (l, 0, j)),
        pl.BlockSpec((None, 1, FF_BLK), lambda i, j: (l, 0, j)),
        pl.BlockSpec((None, FF_BLK, D_MODEL), lambda i, j: (l, j, 0)),
        pl.BlockSpec((None, 1, D_MODEL), lambda i, j: (l, 0, 0)),
        pl.BlockSpec((None, 1, D_MODEL), lambda i, j: (l, 0, 0)),
    ]
    return pl.pallas_call(
        functools.partial(_ffn_kernel, drop_rows=drop_rows),
        grid=(nb, N_FF_BLK),
        in_specs=in_specs,
        out_specs=pl.BlockSpec((1, out_rows, D_MODEL), lambda i, j: (i, 0, 0)),
        out_shape=jax.ShapeDtypeStruct((nb, out_rows, D_MODEL), F32),
        scratch_shapes=[
            pltpu.VMEM((L_TOT, D_MODEL), F32),
            pltpu.VMEM((L_TOT, D_MODEL), BF16),
            pltpu.VMEM((FF_BLK // LANES, L_TOT + 16, LANES), F32),
            pltpu.VMEM((L_TOT, FF_BLK), F32),
            pltpu.VMEM((L_TOT, FF_BLK), BF16),
        ],
        compiler_params=pltpu.CompilerParams(
            dimension_semantics=("arbitrary", "arbitrary"), vmem_limit_bytes=VMEM_LIMIT),
        name=f"ffn{l}",
    )(h, p["w_up"], p["w_up"], p["conv_f_w"], p["conv_f_b"], p["w_down"], p["ln2_g"], p["ln2_b"])


def _block_diag(w, n):
    d = w.shape[-1]
    eye = jnp.eye(n, dtype=w.dtype)
    full = w[..., :, :, None, :] * eye[:, None, :, None]
    return full.reshape(w.shape[:-3] + (n * d, n * d))


def _prepare_params(w_in, b_in, conv_a_w, conv_a_b, norm_a_g, norm_a_b, conv_b_w, conv_b_b,
                    gate_a_w, gate_a_b, gate_x_w, gate_x_b, lru_lambda, pool_w, pool_b, pool_scale,
                    w_out, b_out, ln1_g, ln1_b, w_up, conv_f_w, conv_f_b, w_down, ln2_g, ln2_b):
    depth = w_in.shape[0]
    heads_per_blk = LANE_BLK // (D_LRU // N_LRU_HEADS)
    n_blk = D_LRU // LANE_BLK

    def gate_blocks(w):
        w = w.reshape(depth, 2, n_blk, heads_per_blk, w.shape[-2], w.shape[-1])
        return _block_diag(w, heads_per_blk)

    row = lambda a: a.reshape(depth, 1, a.shape[-1])
    w_gate = 0.5 * jnp.concatenate([gate_blocks(gate_a_w), gate_blocks(gate_x_w)], axis=-1)
    bias_blocks = lambda b: b.reshape(depth, 2, n_blk, LANE_BLK)
    gate_b = 0.5 * jnp.concatenate([bias_blocks(gate_a_b), bias_blocks(gate_x_b)], axis=-1)
    return {
        "w_in": w_in.astype(BF16),
        "b_in": row(b_in),
        "conv_a_w": conv_a_w,
        "conv_a_b": row(conv_a_b),
        "norm_a_g": row(norm_a_g),
        "norm_a_b": row(norm_a_b),
        "conv_b_w": conv_b_w.reshape(depth, 2 * CONV_B_WIDTH, D_LRU),
        "conv_b_b": conv_b_b,
        "w_gate": w_gate.astype(BF16),
        "gate_b": gate_b,
        "lru_lambda": lru_lambda,
        "pool_w": _block_diag(pool_w, pool_w.shape[1]).astype(BF16),
        "pool_b": row(pool_b),
        "pool_scale": row(pool_scale),
        "w_out": w_out.astype(BF16),
        "b_out": row(b_out),
        "ln1_g": row(ln1_g),
        "ln1_b": row(ln1_b),
        "w_up": w_up.astype(BF16),
        "conv_f_w": conv_f_w,
        "conv_f_b": row(conv_f_b),
        "w_down": w_down.astype(BF16),
        "ln2_g": row(ln2_g),
        "ln2_b": row(ln2_b),
    }


def kernel(x, meta_tokens, emb_ln_g, emb_ln_b, w_in, b_in, conv_a_w, conv_a_b, norm_a_g, norm_a_b, conv_b_w, conv_b_b, gate_a_w, gate_a_b, gate_x_w, gate_x_b, lru_lambda, pool_w, pool_b, pool_scale, w_out, b_out, ln1_g, ln1_b, w_up, conv_f_w, conv_f_b, w_down, ln2_g, ln2_b):
    p = _prepare_params(w_in, b_in, conv_a_w, conv_a_b, norm_a_g, norm_a_b, conv_b_w, conv_b_b,
                        gate_a_w, gate_a_b, gate_x_w, gate_x_b, lru_lambda, pool_w, pool_b,
                        pool_scale, w_out, b_out, ln1_g, ln1_b, w_up, conv_f_w, conv_f_b, w_down,
                        ln2_g, ln2_b)
    h = _embed(x, meta_tokens, emb_ln_g.reshape(1, D_MODEL), emb_ln_b.reshape(1, D_MODEL))
    for l in range(DEPTH):
        mix = _heads(h, l, p)
        h = _outproj(h, mix, l, p)
        h = _ffn(h, l, p, drop_rows=N_META if l == DEPTH - 1 else 0)
    return h
```

```python
import functools
import math

import jax
import jax.numpy as jnp
from jax import lax
from jax.experimental import pallas as pl
from jax.experimental.pallas import tpu as pltpu

D_MODEL = 1024
SEQ = 2048
DEPTH = 4
N_META = 16
L_TOT = SEQ + N_META
D_CONV = 256
D_LRU = 512
D_POOL = 256
N_LRU_HEADS = 8
LRU_C = 8.0
CONV_A_WIDTH = 31
CONV_B_WIDTH = 4
D_FF = 2816
D_IN = 2 * D_CONV + 2 * D_LRU + D_POOL
ALPHA = (2.0 * DEPTH) ** 0.25
LN_EPS = 1e-5
GELU_C0 = math.sqrt(2.0 / math.pi)
GELU_C1 = GELU_C0 * 0.044715
LOG2_E = 1.0 / math.log(2.0)
SQRT_GUARD = 1e-30

COL_UG = 0
COL_Y = 2 * D_CONV
COL_X = 2 * D_CONV + D_LRU
COL_V = 2 * D_CONV + 2 * D_LRU

LANES = 128
SUBLANES = 8
ROWS_EW = 48
ROWS_MM = 688
ROWS_SUB = 16
OUTPROJ_SPLIT = 22 * ROWS_SUB
HALO = 16
LANE_BLK = 256
SLABS = LANE_BLK // LANES
SCAN_CHUNK = L_TOT // SUBLANES
SCAN_UNROLL = 6
FF_BLK = 256
N_FF_BLK = D_FF // FF_BLK
VMEM_LIMIT = 60 * 1024 * 1024
HEADS_VMEM_LIMIT = 62 * 1024 * 1024

F32 = jnp.float32
BF16 = jnp.bfloat16


def _row_loop(n_rows, tile, fn, first=0, last=None):
    last = n_rows // tile if last is None else last

    def body(i, c):
        fn(pl.multiple_of(i * tile, tile))
        return c
    lax.fori_loop(first, last, body, 0)


def _layer_norm(x, g, b):
    mu = jnp.mean(x, axis=-1, keepdims=True)
    xc = x - mu
    var = jnp.mean(xc * xc, axis=-1, keepdims=True)
    return xc * lax.rsqrt(var + LN_EPS) * g + b


def _layer_norm_1pass(x, g, b):
    inv_n = 1.0 / x.shape[-1]
    mu = jnp.sum(x, axis=-1, keepdims=True) * inv_n
    ex2 = jnp.sum(x * x, axis=-1, keepdims=True) * inv_n
    return (x - mu) * lax.rsqrt(ex2 - mu * mu + LN_EPS) * g + b


def _gelu_tanh(g):
    return (0.5 * g) * (1.0 + jnp.tanh(g * (GELU_C0 + GELU_C1 * (g * g))))


def _embed_kernel(x_ref, meta_ref, g_ref, b_ref, o_ref):
    g = g_ref[...]
    b = b_ref[...]
    o_ref[0, 0:N_META, :] = _layer_norm(meta_ref[...], g, b)

    def rows(r):
        o_ref[0, pl.ds(N_META + r, 64), :] = _layer_norm(x_ref[0, pl.ds(r, 64), :], g, b)
    _row_loop(SEQ, 64, rows)


def _embed(x, meta, g, b):
    nb = x.shape[0]
    return pl.pallas_call(
        _embed_kernel,
        grid=(nb,),
        in_specs=[
            pl.BlockSpec((1, SEQ, D_MODEL), lambda i: (i, 0, 0)),
            pl.BlockSpec((N_META, D_MODEL), lambda i: (0, 0)),
            pl.BlockSpec((1, D_MODEL), lambda i: (0, 0)),
            pl.BlockSpec((1, D_MODEL), lambda i: (0, 0)),
        ],
        out_specs=pl.BlockSpec((1, L_TOT, D_MODEL), lambda i: (i, 0, 0)),
        out_shape=jax.ShapeDtypeStruct((nb, L_TOT, D_MODEL), F32),
        compiler_params=pltpu.CompilerParams(
            dimension_semantics=("arbitrary",), vmem_limit_bytes=VMEM_LIMIT),
        name="embed",
    )(x, meta, g, b)


def _heads_kernel(h_ref, w_in_ref, b_in_ref, caw_ref, cab_ref, nag_ref, nab_ref,
                  cbw_ref, cbb_ref, wg_ref, gb_ref, lam_ref,
                  pw_ref, pb_ref, ps_ref,
                  o_ref, xa_ref, xv_ref, xx_ref, xc_ref, gt_ref, ab_ref, gy_ref, lhs_ref):
    R = ROWS_EW
    RM = ROWS_MM
    RS = ROWS_SUB
    n_tiles = L_TOT // R

    for ref in (xa_ref, xv_ref, xx_ref):
        for jb in range(ref.shape[0]):
            ref[jb, 0:HALO, :] = jnp.zeros((HALO, LANES), F32)
            ref[jb, L_TOT + HALO:L_TOT + 2 * HALO, :] = jnp.zeros((HALO, LANES), F32)

    def proj(c0, n):
        return (jnp.dot(lhs_ref[...], w_in_ref[:, c0:c0 + n], preferred_element_type=F32)
                + b_in_ref[:, c0:c0 + n])

    def to_slabs(ref, r, val):
        for jb in range(val.shape[1] // LANES):
            ref[jb, pl.ds(HALO + r, RM), :] = val[:, jb * LANES:(jb + 1) * LANES]

    def a_proj(r):
        lhs_ref[...] = h_ref[0, pl.ds(r, RM), :].astype(BF16)
        ug = proj(COL_UG, 2 * D_CONV)
        hu = 0.5 * ug[:, :D_CONV]
        to_slabs(xa_ref, r, hu + hu * jnp.tanh(0.5 * ug[:, D_CONV:]))
    _row_loop(L_TOT, RM, a_proj)

    def conv_a_rows(r, wins, t0, n):
        accs = []
        for jb, win in enumerate(wins):
            cols = slice(jb * LANES, (jb + 1) * LANES)
            acc = cab_ref[:, cols] + caw_ref[0:1, cols] * win[pl.ds(t0 + 1, n), :]
            for k in range(1, CONV_A_WIDTH):
                acc = acc + caw_ref[k:k + 1, cols] * win[pl.ds(t0 + k + 1, n), :]
            accs.append(acc)
        inv_n = 1.0 / D_CONV
        mu = sum(jnp.sum(a, axis=-1, keepdims=True) for a in accs) * inv_n
        ex2 = sum(jnp.sum(a * a, axis=-1, keepdims=True) for a in accs) * inv_n
        inv = lax.rsqrt(ex2 - mu * mu + LN_EPS)
        for jb, acc in enumerate(accs):
            cols = slice(jb * LANES, (jb + 1) * LANES)
            hy = 0.5 * ((acc - mu) * inv * nag_ref[:, cols] + nab_ref[:, cols])
            o_ref[0, pl.ds(r + t0, n), cols] = (hy + hy * jnp.tanh(hy)).astype(BF16)

    def a_conv_and_proj(r):
        lhs_ref[...] = h_ref[0, pl.ds(r, RM), :].astype(BF16)
        wins = [xa_ref.at[jb, pl.ds(r, RM + 2 * HALO), :] for jb in range(D_CONV // LANES)]

        def x_piece(half):
            val = proj(COL_X + half * LANE_BLK, LANE_BLK)
            for jb in range(SLABS):
                xx_ref[SLABS * half + jb, pl.ds(HALO + r, RM), :] = val[:, jb * LANES:(jb + 1) * LANES]

        def y_piece(half):
            cols = slice(half * LANE_BLK, (half + 1) * LANE_BLK)
            y = proj(COL_Y + half * LANE_BLK, LANE_BLK)
            gy_ref[pl.ds(r, RM), cols] = _gelu_tanh(y).astype(BF16)

        pieces = [lambda: x_piece(0), lambda: x_piece(1), lambda: y_piece(0), lambda: y_piece(1),
                  lambda: to_slabs(xv_ref, r, proj(COL_V, D_POOL))]
        groups = [(t0, min(R, RM - t0)) for t0 in range(0, RM, R)]
        bounds = [len(groups) * q // len(pieces) for q in range(len(pieces) + 1)]
        for q, piece in enumerate(pieces):
            piece()
            for t0, n in groups[bounds[q]:bounds[q + 1]]:
                conv_a_rows(r, wins, t0, n)
    _row_loop(L_TOT, RM, a_conv_and_proj)

    lane = lax.broadcasted_iota(jnp.int32, (1, LANES), 1)
    low = lane < 64

    def c_pool(r, edge):
        def taps(jb, lo, hi):
            win = xv_ref.at[jb, pl.ds(r, R + 2 * HALO), :]
            return [win[pl.ds(HALO + d, R), :] for d in range(lo, hi)]
        x0 = taps(0, -2, 2)
        s2 = x0[1] + x0[2]
        s4 = s2 + x0[0] + x0[3]
        x1 = taps(1, -8, 8)
        s8 = sum(x1[5:12], x1[4])
        s16 = sum(x1[0:4] + x1[12:16], s8)
        sums = (jnp.where(low, s2, s4), jnp.where(low, s8, s16))
        halves = ((1, 2), (4, 8))
        selfs = (x0[2], x1[8])
        for jb in range(2):
            if edge:
                t = r + lax.broadcasted_iota(jnp.int32, (R, LANES), 0)
                half = jnp.where(low, halves[jb][0], halves[jb][1])
                cnt = jnp.minimum(t + half, L_TOT) - jnp.maximum(t - half, 0)
                mean = sums[jb] / cnt.astype(F32)
            else:
                mean = sums[jb] * jnp.where(low, 0.5 / halves[jb][0], 0.5 / halves[jb][1])
            xc_ref[pl.ds(r, R), jb * LANES:(jb + 1) * LANES] = mean - selfs[jb]
    c_pool(0, True)
    _row_loop(L_TOT, R, lambda r: c_pool(r, False), first=1, last=n_tiles - 1)
    c_pool((n_tiles - 1) * R, True)

    def c_out(r):
        m = xc_ref[pl.ds(r, RM), :].astype(BF16)
        c = jnp.dot(m, pw_ref[...], preferred_element_type=F32) + pb_ref[...]
        o_ref[0, pl.ds(r, RM), D_CONV + D_LRU:D_MODEL] = (c * ps_ref[...]).astype(BF16)
    _row_loop(L_TOT, RM, c_out)

    sub = lax.broadcasted_iota(jnp.int32, (SUBLANES, LANES), 0)

    def slot(d, is_b, jb):
        return 4 * d + 2 * is_b + jb

    for cb in range(D_LRU // LANE_BLK):
        c0 = cb * LANE_BLK

        for d in range(2):
            z = -lam_ref[d:d + 1, c0:c0 + LANE_BLK]
            softplus = jnp.maximum(z, 0.0) + jnp.log1p(jnp.exp(-jnp.abs(z)))
            half_decay_log2 = (-0.5 * LRU_C * LOG2_E) * softplus

            for r in range(0, L_TOT, RM):
                for t0 in range(r, r + RM, RS):
                    for jb in range(SLABS):
                        cols = slice(c0 + jb * LANES, c0 + (jb + 1) * LANES)
                        acc = cbb_ref[d:d + 1, cols]
                        for k in range(CONV_B_WIDTH):
                            off = HALO + t0 + (k - (CONV_B_WIDTH - 1) if d == 0 else k)
                            w_k = cbw_ref[d * CONV_B_WIDTH + k:d * CONV_B_WIDTH + k + 1, cols]
                            acc = acc + w_k * xx_ref[SLABS * cb + jb, off:off + RS, :]
                        xc_ref[t0:t0 + RS, jb * LANES:(jb + 1) * LANES] = acc

                gt_ref[r:r + RM, :] = (
                    jnp.dot(xc_ref[r:r + RM, :].astype(BF16), wg_ref[d, cb], preferred_element_type=F32)
                    + gb_ref[d, cb:cb + 1, :])

                for t0 in range(r, r + RM, RS):
                    rows = slice(t0, t0 + RS)
                    for jb in range(SLABS):
                        lc = slice(jb * LANES, (jb + 1) * LANES)
                        tr = jnp.tanh(gt_ref[rows, lc])
                        ti = jnp.tanh(gt_ref[rows, LANE_BLK + jb * LANES:LANE_BLK + (jb + 1) * LANES])
                        hd = half_decay_log2[:, lc]
                        a = jnp.exp2(hd + hd * tr)
                        ab_ref[slot(d, 0, jb), rows, :] = a
                        s = 1.0 - a * a
                        ab_ref[slot(d, 1, jb), rows, :] = (
                            (s * lax.rsqrt(jnp.maximum(s, SQRT_GUARD)))
                            * ((0.5 + 0.5 * ti) * xc_ref[rows, lc]))

        chains = [(d, jb) for d in range(2) for jb in range(SLABS)]
        h_out = (xa_ref, xv_ref)

        def chunk_rows(d, step, base=0):
            t2 = step if d == 0 else SCAN_CHUNK - 1 - step
            return pl.ds(base + t2, SUBLANES, stride=SCAN_CHUNK)

        def sweep_local(i, carry):
            hs, ps = list(carry[0]), list(carry[1])
            for u in range(SCAN_UNROLL):
                for c, (d, jb) in enumerate(chains):
                    rows = chunk_rows(d, i * SCAN_UNROLL + u)
                    a = ab_ref[slot(d, 0, jb), rows, :]
                    hs[c] = a * hs[c] + ab_ref[slot(d, 1, jb), rows, :]
                    ps[c] = a * ps[c]
            return tuple(hs), tuple(ps)
        zeros = tuple(jnp.zeros((SUBLANES, LANES), F32) for _ in chains)
        ones = tuple(jnp.ones((SUBLANES, LANES), F32) for _ in chains)
        h_fin, p_fin = lax.fori_loop(0, SCAN_CHUNK // SCAN_UNROLL, sweep_local, (zeros, ones))

        h_in = []
        for c, (d, jb) in enumerate(chains):
            first = (sub == 0) if d == 0 else (sub == SUBLANES - 1)
            shift = 1 if d == 0 else SUBLANES - 1
            init = jnp.zeros((SUBLANES, LANES), F32)
            for _ in range(SUBLANES - 1):
                true_fin = h_fin[c] + p_fin[c] * init
                init = jnp.where(first, 0.0, pltpu.roll(true_fin, shift, 0))
            h_in.append(init)

        def sweep_true(i, carry):
            hs = list(carry)
            for u in range(SCAN_UNROLL):
                for c, (d, jb) in enumerate(chains):
                    rows = chunk_rows(d, i * SCAN_UNROLL + u)
                    hs[c] = ab_ref[slot(d, 0, jb), rows, :] * hs[c] + ab_ref[slot(d, 1, jb), rows, :]
                    h_out[d][jb, chunk_rows(d, i * SCAN_UNROLL + u, HALO), :] = hs[c]
            return tuple(hs)
        lax.fori_loop(0, SCAN_CHUNK // SCAN_UNROLL, sweep_true, tuple(h_in))

        def b_out(r, c0=c0):
            rows = pl.ds(r, R)
            hrows = pl.ds(HALO + r, R)
            for jb in range(SLABS):
                gy = gy_ref[rows, c0 + jb * LANES:c0 + (jb + 1) * LANES].astype(F32)
                oc = D_CONV + c0 + jb * LANES
                o_ref[0, rows, oc:oc + LANES] = (
                    gy * (xa_ref[jb, hrows, :] + xv_ref[jb, hrows, :])).astype(BF16)
        _row_loop(L_TOT, R, b_out)


def _const_spec(block, index):
    return pl.BlockSpec(block, lambda *_: index, pipeline_mode=pl.Buffered(1))


def _heads(h, l, p):
    nb = h.shape[0]
    z2 = (l, 0, 0)
    in_specs = [
        pl.BlockSpec((1, L_TOT, D_MODEL), lambda i: (i, 0, 0)),
        _const_spec((None, D_MODEL, D_IN), z2),
        _const_spec((None, 1, D_IN), z2),
        _const_spec((None, CONV_A_WIDTH, D_CONV), z2),
        _const_spec((None, 1, D_CONV), z2),
        _const_spec((None, 1, D_CONV), z2),
        _const_spec((None, 1, D_CONV), z2),
        _const_spec((None, 2 * CONV_B_WIDTH, D_LRU), z2),
        _const_spec((None, 2, D_LRU), z2),
        _const_spec((None, 2, D_LRU // LANE_BLK, LANE_BLK, 2 * LANE_BLK), (l, 0, 0, 0, 0)),
        _const_spec((None, 2, D_LRU // LANE_BLK, 2 * LANE_BLK), (l, 0, 0, 0)),
        _const_spec((None, 2, D_LRU), z2),
        _const_spec((None, D_POOL, D_POOL), z2),
        _const_spec((None, 1, D_POOL), z2),
        _const_spec((None, 1, D_POOL), z2),
    ]
    return pl.pallas_call(
        _heads_kernel,
        grid=(nb,),
        in_specs=in_specs,
        out_specs=pl.BlockSpec((1, L_TOT, D_MODEL), lambda i: (i, 0, 0)),
        out_shape=jax.ShapeDtypeStruct((nb, L_TOT, D_MODEL), BF16),
        scratch_shapes=[
            pltpu.VMEM((SLABS, L_TOT + 2 * HALO, LANES), F32),
            pltpu.VMEM((SLABS, L_TOT + 2 * HALO, LANES), F32),
            pltpu.VMEM((2 * SLABS, L_TOT + 2 * HALO, LANES), F32),
            pltpu.VMEM((L_TOT, LANE_BLK), F32),
            pltpu.VMEM((L_TOT, 2 * LANE_BLK), F32),
            pltpu.VMEM((4 * SLABS, L_TOT, LANES), F32),
            pltpu.VMEM((L_TOT, D_LRU), BF16),
            pltpu.VMEM((ROWS_MM, D_MODEL), BF16),
        ],
        compiler_params=pltpu.CompilerParams(
            dimension_semantics=("arbitrary",), vmem_limit_bytes=HEADS_VMEM_LIMIT),
        name=f"heads{l}",
    )(h, p["w_in"], p["b_in"], p["conv_a_w"], p["conv_a_b"], p["norm_a_g"], p["norm_a_b"],
      p["conv_b_w"], p["conv_b_b"], p["w_gate"], p["gate_b"], p["lru_lambda"],
      p["pool_w"], p["pool_b"], p["pool_scale"])


def _outproj_kernel(h_ref, mix_ref, w_ref, b_ref, g_ref, beta_ref, o_ref):
    for lo, hi in ((0, OUTPROJ_SPLIT), (OUTPROJ_SPLIT, ROWS_MM)):
        o_ref[0, lo:hi, :] = (
            ALPHA * h_ref[0, lo:hi, :]
            + jnp.dot(mix_ref[0, lo:hi, :], w_ref[...], preferred_element_type=F32) + b_ref[...])
        for t0 in range(lo, hi, ROWS_SUB):
            o_ref[0, t0:t0 + ROWS_SUB, :] = _layer_norm_1pass(
                o_ref[0, t0:t0 + ROWS_SUB, :], g_ref[...], beta_ref[...])


def _outproj(h, mix, l, p):
    nb = h.shape[0]
    tile = lambda i, t: (i, t, 0)
    return pl.pallas_call(
        _outproj_kernel,
        grid=(nb, L_TOT // ROWS_MM),
        in_specs=[
            pl.BlockSpec((1, ROWS_MM, D_MODEL), tile),
            pl.BlockSpec((1, ROWS_MM, D_MODEL), tile),
            _const_spec((None, D_MODEL, D_MODEL), (l, 0, 0)),
            _const_spec((None, 1, D_MODEL), (l, 0, 0)),
            _const_spec((None, 1, D_MODEL), (l, 0, 0)),
            _const_spec((None, 1, D_MODEL), (l, 0, 0)),
        ],
        out_specs=pl.BlockSpec((1, ROWS_MM, D_MODEL), tile),
        out_shape=jax.ShapeDtypeStruct((nb, L_TOT, D_MODEL), F32),
        compiler_params=pltpu.CompilerParams(
            dimension_semantics=("arbitrary", "arbitrary"), vmem_limit_bytes=VMEM_LIMIT),
        name=f"outproj{l}",
    )(h, mix, p["w_out"], p["b_out"], p["ln1_g"], p["ln1_b"])


def _ffn_kernel(h_ref, wg_ref, wv_ref, cfw_ref, cfb_ref, wd_ref, lng_ref, lnb_ref,
                o_ref, acc_ref, hb_ref, gp_ref, v_ref, p_ref, *, drop_rows):
    RM = ROWS_MM
    RS = ROWS_SUB
    T2 = 2 * RM
    n_lane_blk = FF_BLK // LANES
    j = pl.program_id(1)

    def init(r):
        for t0 in range(r, r + RM, RS):
            x = h_ref[0, t0:t0 + RS, :]
            hb_ref[t0:t0 + RS, :] = x.astype(BF16)
            acc_ref[t0:t0 + RS, :] = ALPHA * x

    def up(r):
        g = jnp.dot(hb_ref[r:r + RM, :], wg_ref[...], preferred_element_type=F32)
        for jb in range(n_lane_blk):
            gp_ref[jb, 8 + r:8 + r + RM, :] = g[:, jb * LANES:(jb + 1) * LANES]
        v_ref[r:r + RM, :] = jnp.dot(hb_ref[r:r + RM, :], wv_ref[...], preferred_element_type=F32)

    def glu_down(r):
        for t0 in range(r, r + RM, RS):
            for jb in range(n_lane_blk):
                cols = slice(jb * LANES, (jb + 1) * LANES)
                g = (cfw_ref[0:1, cols] * gp_ref[jb, t0 + 7:t0 + 7 + RS, :]
                     + cfw_ref[1:2, cols] * gp_ref[jb, t0 + 8:t0 + 8 + RS, :]
                     + cfw_ref[2:3, cols] * gp_ref[jb, t0 + 9:t0 + 9 + RS, :]
                     + cfb_ref[:, cols])
                p_ref[t0:t0 + RS, cols] = (_gelu_tanh(g) * v_ref[t0:t0 + RS, cols]).astype(BF16)
        acc_ref[r:r + RM, :] = acc_ref[r:r + RM, :] + jnp.dot(
            p_ref[r:r + RM, :], wd_ref[...], preferred_element_type=F32)

    @pl.when(j == 0)
    def _first():
        for jb in range(n_lane_blk):
            gp_ref[jb, 0:8, :] = jnp.zeros((8, LANES), F32)
            gp_ref[jb, L_TOT + 8:L_TOT + 16, :] = jnp.zeros((8, LANES), F32)
        init(0)
        up(0)
        init(RM)
        up(RM)
        glu_down(0)
        init(T2)
        up(T2)
        glu_down(RM)
        glu_down(T2)

    @pl.when(j > 0)
    def _rest():
        up(0)
        up(RM)
        glu_down(0)
        up(T2)
        glu_down(RM)
        glu_down(T2)

    @pl.when(j == N_FF_BLK - 1)
    def _finish():
        for t0 in range(drop_rows, L_TOT, RS):
            o_ref[0, t0 - drop_rows:t0 - drop_rows + RS, :] = _layer_norm_1pass(
                acc_ref[t0:t0 + RS, :], lng_ref[...], lnb_ref[...])


def _ffn(h, l, p, drop_rows):
    nb = h.shape[0]
    out_rows = L_TOT - drop_rows
    in_specs = [
        pl.BlockSpec((1, L_TOT, D_MODEL), lambda i, j: (i, 0, 0)),
        pl.BlockSpec((None, D_MODEL, FF_BLK), lambda i, j: (l, 0, j)),
        pl.BlockSpec((None, D_MODEL, FF_BLK), lambda i, j: (l, 0, j + N_FF_BLK)),
        pl.BlockSpec((None, 3, FF_BLK), lambda i, j: (l, 0, j)),
        pl.BlockSpec((None, 1, FF_BLK), lambda i, j: (l, 0, j)),
        pl.BlockSpec((None, FF_BLK, D_MODEL), lambda i, j: (l, j, 0)),
        pl.BlockSpec((None, 1, D_MODEL), lambda i, j: (l, 0, 0)),
        pl.BlockSpec((None, 1, D_MODEL), lambda i, j: (l, 0, 0)),
    ]
    return pl.pallas_call(
        functools.partial(_ffn_kernel, drop_rows=drop_rows),
        grid=(nb, N_FF_BLK),
        in_specs=in_specs,
        out_specs=pl.BlockSpec((1, out_rows, D_MODEL), lambda i, j: (i, 0, 0)),
        out_shape=jax.ShapeDtypeStruct((nb, out_rows, D_MODEL), F32),
        scratch_shapes=[
            pltpu.VMEM((L_TOT, D_MODEL), F32),
            pltpu.VMEM((L_TOT, D_MODEL), BF16),
            pltpu.VMEM((FF_BLK // LANES, L_TOT + 16, LANES), F32),
            pltpu.VMEM((L_TOT, FF_BLK), F32),
            pltpu.VMEM((L_TOT, FF_BLK), BF16),
        ],
        compiler_params=pltpu.CompilerParams(
            dimension_semantics=("arbitrary", "arbitrary"), vmem_limit_bytes=VMEM_LIMIT),
        name=f"ffn{l}",
    )(h, p["w_up"], p["w_up"], p["conv_f_w"], p["conv_f_b"], p["w_down"], p["ln2_g"], p["ln2_b"])


def _block_diag(w, n):
    d = w.shape[-1]
    eye = jnp.eye(n, dtype=w.dtype)
    full = w[..., :, :, None, :] * eye[:, None, :, None]
    return full.reshape(w.shape[:-3] + (n * d, n * d))


def _prepare_params(w_in, b_in, conv_a_w, conv_a_b, norm_a_g, norm_a_b, conv_b_w, conv_b_b,
                    gate_a_w, gate_a_b, gate_x_w, gate_x_b, lru_lambda, pool_w, pool_b, pool_scale,
                    w_out, b_out, ln1_g, ln1_b, w_up, conv_f_w, conv_f_b, w_down, ln2_g, ln2_b):
    depth = w_in.shape[0]
    heads_per_blk = LANE_BLK // (D_LRU // N_LRU_HEADS)
    n_blk = D_LRU // LANE_BLK

    def gate_blocks(w):
        w = w.reshape(depth, 2, n_blk, heads_per_blk, w.shape[-2], w.shape[-1])
        return _block_diag(w, heads_per_blk)

    row = lambda a: a.reshape(depth, 1, a.shape[-1])
    w_gate = 0.5 * jnp.concatenate([gate_blocks(gate_a_w), gate_blocks(gate_x_w)], axis=-1)
    bias_blocks = lambda b: b.reshape(depth, 2, n_blk, LANE_BLK)
    gate_b = 0.5 * jnp.concatenate([bias_blocks(gate_a_b), bias_blocks(gate_x_b)], axis=-1)
    return {
        "w_in": w_in.astype(BF16),
        "b_in": row(b_in),
        "conv_a_w": conv_a_w,
        "conv_a_b": row(conv_a_b),
        "norm_a_g": row(norm_a_g),
        "norm_a_b": row(norm_a_b),
        "conv_b_w": conv_b_w.reshape(depth, 2 * CONV_B_WIDTH, D_LRU),
        "conv_b_b": conv_b_b,
        "w_gate": w_gate.astype(BF16),
        "gate_b": gate_b,
        "lru_lambda": lru_lambda,
        "pool_w": _block_diag(pool_w, pool_w.shape[1]).astype(BF16),
        "pool_b": row(pool_b),
        "pool_scale": row(pool_scale),
        "w_out": w_out.astype(BF16),
        "b_out": row(b_out),
        "ln1_g": row(ln1_g),
        "ln1_b": row(ln1_b),
        "w_up": w_up.astype(BF16),
        "conv_f_w": conv_f_w,
        "conv_f_b": row(conv_f_b),
        "w_down": w_down.astype(BF16),
        "ln2_g": row(ln2_g),
        "ln2_b": row(ln2_b),
    }


def kernel(x, meta_tokens, emb_ln_g, emb_ln_b, w_in, b_in, conv_a_w, conv_a_b, norm_a_g, norm_a_b, conv_b_w, conv_b_b, gate_a_w, gate_a_b, gate_x_w, gate_x_b, lru_lambda, pool_w, pool_b, pool_scale, w_out, b_out, ln1_g, ln1_b, w_up, conv_f_w, conv_f_b, w_down, ln2_g, ln2_b):
    p = _prepare_params(w_in, b_in, conv_a_w, conv_a_b, norm_a_g, norm_a_b, conv_b_w, conv_b_b,
                        gate_a_w, gate_a_b, gate_x_w, gate_x_b, lru_lambda, pool_w, pool_b,
                        pool_scale, w_out, b_out, ln1_g, ln1_b, w_up, conv_f_w, conv_f_b, w_down,
                        ln2_g, ln2_b)
    h = _embed(x, meta_tokens, emb_ln_g.reshape(1, D_MODEL), emb_ln_b.reshape(1, D_MODEL))
    for l in range(DEPTH):
        mix = _heads(h, l, p)
        h = _outproj(h, mix, l, p)
        h = _ffn(h, l, p, drop_rows=N_META if l == DEPTH - 1 else 0)
    return h
```

```python
import functools
import math

import jax
import jax.numpy as jnp
from jax import lax
from jax.experimental import pallas as pl
from jax.experimental.pallas import tpu as pltpu

D_MODEL = 1024
SEQ = 2048
DEPTH = 4
N_META = 16
L_TOT = SEQ + N_META
D_CONV = 256
D_LRU = 512
D_POOL = 256
N_LRU_HEADS = 8
LRU_C = 8.0
CONV_A_WIDTH = 31
CONV_B_WIDTH = 4
D_FF = 2816
D_IN = 2 * D_CONV + 2 * D_LRU + D_POOL
ALPHA = (2.0 * DEPTH) ** 0.25
LN_EPS = 1e-5
GELU_C0 = math.sqrt(2.0 / math.pi)
GELU_C1 = GELU_C0 * 0.044715
LOG2_E = 1.0 / math.log(2.0)
SQRT_GUARD = 1e-30

COL_UG = 0
COL_Y = 2 * D_CONV
COL_X = 2 * D_CONV + D_LRU
COL_V = 2 * D_CONV + 2 * D_LRU

LANES = 128
SUBLANES = 8
ROWS_EW = 48
ROWS_MM = 688
ROWS_SUB = 16
OUTPROJ_SPLIT = 22 * ROWS_SUB
HALO = 16
LANE_BLK = 256
SLABS = LANE_BLK // LANES
SCAN_CHUNK = L_TOT // SUBLANES
SCAN_UNROLL = 6
FF_BLK = 256
N_FF_BLK = D_FF // FF_BLK
VMEM_LIMIT = 60 * 1024 * 1024
HEADS_VMEM_LIMIT = 62 * 1024 * 1024

F32 = jnp.float32
BF16 = jnp.bfloat16


def _row_loop(n_rows, tile, fn, first=0, last=None):
    last = n_rows // tile if last is None else last

    def body(i, c):
        fn(pl.multiple_of(i * tile, tile))
        return c
    lax.fori_loop(first, last, body, 0)


def _layer_norm(x, g, b):
    mu = jnp.mean(x, axis=-1, keepdims=True)
    xc = x - mu
    var = jnp.mean(xc * xc, axis=-1, keepdims=True)
    return xc * lax.rsqrt(var + LN_EPS) * g + b


def _layer_norm_1pass(x, g, b):
    inv_n = 1.0 / x.shape[-1]
    mu = jnp.sum(x, axis=-1, keepdims=True) * inv_n
    ex2 = jnp.sum(x * x, axis=-1, keepdims=True) * inv_n
    return (x - mu) * lax.rsqrt(ex2 - mu * mu + LN_EPS) * g + b


def _gelu_tanh(g):
    return (0.5 * g) * (1.0 + jnp.tanh(g * (GELU_C0 + GELU_C1 * (g * g))))


def _embed_kernel(x_ref, meta_ref, g_ref, b_ref, o_ref):
    g = g_ref[...]
    b = b_ref[...]
    o_ref[0, 0:N_META, :] = _layer_norm_1pass(meta_ref[...], g, b)
    for t0 in range(0, SEQ, ROWS_SUB):
        o_ref[0, N_META + t0:N_META + t0 + ROWS_SUB, :] = _layer_norm_1pass(
            x_ref[0, t0:t0 + ROWS_SUB, :], g, b)


def _embed(x, meta, g, b):
    nb = x.shape[0]
    return pl.pallas_call(
        _embed_kernel,
        grid=(nb,),
        in_specs=[
            pl.BlockSpec((1, SEQ, D_MODEL), lambda i: (i, 0, 0)),
            pl.BlockSpec((N_META, D_MODEL), lambda i: (0, 0)),
            pl.BlockSpec((1, D_MODEL), lambda i: (0, 0)),
            pl.BlockSpec((1, D_MODEL), lambda i: (0, 0)),
        ],
        out_specs=pl.BlockSpec((1, L_TOT, D_MODEL), lambda i: (i, 0, 0)),
        out_shape=jax.ShapeDtypeStruct((nb, L_TOT, D_MODEL), F32),
        compiler_params=pltpu.CompilerParams(
            dimension_semantics=("arbitrary",), vmem_limit_bytes=VMEM_LIMIT),
        name="embed",
    )(x, meta, g, b)


def _heads_kernel(h_ref, w_in_ref, b_in_ref, caw_ref, cab_ref, nag_ref, nab_ref,
                  cbw_ref, cbb_ref, wg_ref, gb_ref, lam_ref,
                  pw_ref, pb_ref, ps_ref,
                  o_ref, xa_ref, xv_ref, xx_ref, xc_ref, gt_ref, ab_ref, gy_ref, lhs_ref):
    R = ROWS_EW
    RM = ROWS_MM
    RS = ROWS_SUB
    n_tiles = L_TOT // R

    for ref in (xa_ref, xv_ref, xx_ref):
        for jb in range(ref.shape[0]):
            ref[jb, 0:HALO, :] = jnp.zeros((HALO, LANES), F32)
            ref[jb, L_TOT + HALO:L_TOT + 2 * HALO, :] = jnp.zeros((HALO, LANES), F32)

    def proj(c0, n):
        return (jnp.dot(lhs_ref[...], w_in_ref[:, c0:c0 + n], preferred_element_type=F32)
                + b_in_ref[:, c0:c0 + n])

    def to_slabs(ref, r, val):
        for jb in range(val.shape[1] // LANES):
            ref[jb, pl.ds(HALO + r, RM), :] = val[:, jb * LANES:(jb + 1) * LANES]

    def a_proj(r):
        lhs_ref[...] = h_ref[0, pl.ds(r, RM), :].astype(BF16)
        ug = proj(COL_UG, 2 * D_CONV)
        hu = 0.5 * ug[:, :D_CONV]
        to_slabs(xa_ref, r, hu + hu * jnp.tanh(0.5 * ug[:, D_CONV:]))
    _row_loop(L_TOT, RM, a_proj)

    def conv_a_rows(r, wins, t0, n):
        accs = []
        for jb, win in enumerate(wins):
            cols = slice(jb * LANES, (jb + 1) * LANES)
            acc = cab_ref[:, cols] + caw_ref[0:1, cols] * win[pl.ds(t0 + 1, n), :]
            for k in range(1, CONV_A_WIDTH):
                acc = acc + caw_ref[k:k + 1, cols] * win[pl.ds(t0 + k + 1, n), :]
            accs.append(acc)
        inv_n = 1.0 / D_CONV
        mu = sum(jnp.sum(a, axis=-1, keepdims=True) for a in accs) * inv_n
        ex2 = sum(jnp.sum(a * a, axis=-1, keepdims=True) for a in accs) * inv_n
        inv = lax.rsqrt(ex2 - mu * mu + LN_EPS)
        for jb, acc in enumerate(accs):
            cols = slice(jb * LANES, (jb + 1) * LANES)
            hy = 0.5 * ((acc - mu) * inv * nag_ref[:, cols] + nab_ref[:, cols])
            o_ref[0, pl.ds(r + t0, n), cols] = (hy + hy * jnp.tanh(hy)).astype(BF16)

    def a_conv_and_proj(r):
        lhs_ref[...] = h_ref[0, pl.ds(r, RM), :].astype(BF16)
        wins = [xa_ref.at[jb, pl.ds(r, RM + 2 * HALO), :] for jb in range(D_CONV // LANES)]

        def x_piece(half):
            val = proj(COL_X + half * LANE_BLK, LANE_BLK)
            for jb in range(SLABS):
                xx_ref[SLABS * half + jb, pl.ds(HALO + r, RM), :] = val[:, jb * LANES:(jb + 1) * LANES]

        def y_piece(half):
            cols = slice(half * LANE_BLK, (half + 1) * LANE_BLK)
            y = proj(COL_Y + half * LANE_BLK, LANE_BLK)
            gy_ref[pl.ds(r, RM), cols] = _gelu_tanh(y).astype(BF16)

        pieces = [lambda: x_piece(0), lambda: x_piece(1), lambda: y_piece(0), lambda: y_piece(1),
                  lambda: to_slabs(xv_ref, r, proj(COL_V, D_POOL))]
        groups = [(t0, min(R, RM - t0)) for t0 in range(0, RM, R)]
        bounds = [len(groups) * q // len(pieces) for q in range(len(pieces) + 1)]
        for q, piece in enumerate(pieces):
            piece()
            for t0, n in groups[bounds[q]:bounds[q + 1]]:
                conv_a_rows(r, wins, t0, n)
    _row_loop(L_TOT, RM, a_conv_and_proj)

    lane = lax.broadcasted_iota(jnp.int32, (1, LANES), 1)
    low = lane < 64

    def c_pool(r, edge):
        def taps(jb, lo, hi):
            win = xv_ref.at[jb, pl.ds(r, R + 2 * HALO), :]
            return [win[pl.ds(HALO + d, R), :] for d in range(lo, hi)]
        x0 = taps(0, -2, 2)
        s2 = x0[1] + x0[2]
        s4 = s2 + x0[0] + x0[3]
        x1 = taps(1, -8, 8)
        s8 = sum(x1[5:12], x1[4])
        s16 = sum(x1[0:4] + x1[12:16], s8)
        sums = (jnp.where(low, s2, s4), jnp.where(low, s8, s16))
        halves = ((1, 2), (4, 8))
        selfs = (x0[2], x1[8])
        for jb in range(2):
            if edge:
                t = r + lax.broadcasted_iota(jnp.int32, (R, LANES), 0)
                half = jnp.where(low, halves[jb][0], halves[jb][1])
                cnt = jnp.minimum(t + half, L_TOT) - jnp.maximum(t - half, 0)
                mean = sums[jb] / cnt.astype(F32)
            else:
                mean = sums[jb] * jnp.where(low, 0.5 / halves[jb][0], 0.5 / halves[jb][1])
            xc_ref[pl.ds(r, R), jb * LANES:(jb + 1) * LANES] = mean - selfs[jb]
    c_pool(0, True)
    _row_loop(L_TOT, R, lambda r: c_pool(r, False), first=1, last=n_tiles - 1)
    c_pool((n_tiles - 1) * R, True)

    def c_out(r):
        m = xc_ref[pl.ds(r, RM), :].astype(BF16)
        c = jnp.dot(m, pw_ref[...], preferred_element_type=F32) + pb_ref[...]
        o_ref[0, pl.ds(r, RM), D_CONV + D_LRU:D_MODEL] = (c * ps_ref[...]).astype(BF16)
    _row_loop(L_TOT, RM, c_out)

    sub = lax.broadcasted_iota(jnp.int32, (SUBLANES, LANES), 0)

    def slot(d, is_b, jb):
        return 4 * d + 2 * is_b + jb

    for cb in range(D_LRU // LANE_BLK):
        c0 = cb * LANE_BLK

        for d in range(2):
            z = -lam_ref[d:d + 1, c0:c0 + LANE_BLK]
            softplus = jnp.maximum(z, 0.0) + jnp.log1p(jnp.exp(-jnp.abs(z)))
            half_decay_log2 = (-0.5 * LRU_C * LOG2_E) * softplus

            for r in range(0, L_TOT, RM):
                for t0 in range(r, r + RM, RS):
                    for jb in range(SLABS):
                        cols = slice(c0 + jb * LANES, c0 + (jb + 1) * LANES)
                        acc = cbb_ref[d:d + 1, cols]
                        for k in range(CONV_B_WIDTH):
                            off = HALO + t0 + (k - (CONV_B_WIDTH - 1) if d == 0 else k)
                            w_k = cbw_ref[d * CONV_B_WIDTH + k:d * CONV_B_WIDTH + k + 1, cols]
                            acc = acc + w_k * xx_ref[SLABS * cb + jb, off:off + RS, :]
                        xc_ref[t0:t0 + RS, jb * LANES:(jb + 1) * LANES] = acc

                gt_ref[r:r + RM, :] = (
                    jnp.dot(xc_ref[r:r + RM, :].astype(BF16), wg_ref[d, cb], preferred_element_type=F32)
                    + gb_ref[d, cb:cb + 1, :])

                for t0 in range(r, r + RM, RS):
                    rows = slice(t0, t0 + RS)
                    for jb in range(SLABS):
                        lc = slice(jb * LANES, (jb + 1) * LANES)
                        tr = jnp.tanh(gt_ref[rows, lc])
                        ti = jnp.tanh(gt_ref[rows, LANE_BLK + jb * LANES:LANE_BLK + (jb + 1) * LANES])
                        hd = half_decay_log2[:, lc]
                        a = jnp.exp2(hd + hd * tr)
                        ab_ref[slot(d, 0, jb), rows, :] = a
                        s = 1.0 - a * a
                        ab_ref[slot(d, 1, jb), rows, :] = (
                            (s * lax.rsqrt(jnp.maximum(s, SQRT_GUARD)))
                            * ((1.0 + ti) * xc_ref[rows, lc]))

        chains = [(d, jb) for d in range(2) for jb in range(SLABS)]
        h_out = (xa_ref, xv_ref)

        def chunk_rows(d, step, base=0):
            t2 = step if d == 0 else SCAN_CHUNK - 1 - step
            return pl.ds(base + t2, SUBLANES, stride=SCAN_CHUNK)

        def sweep_local(i, carry):
            hs, ps = list(carry[0]), list(carry[1])
            for u in range(SCAN_UNROLL):
                for c, (d, jb) in enumerate(chains):
                    rows = chunk_rows(d, i * SCAN_UNROLL + u)
                    a = ab_ref[slot(d, 0, jb), rows, :]
                    hs[c] = a * hs[c] + ab_ref[slot(d, 1, jb), rows, :]
                    ps[c] = a * ps[c]
            return tuple(hs), tuple(ps)
        zeros = tuple(jnp.zeros((SUBLANES, LANES), F32) for _ in chains)
        ones = tuple(jnp.ones((SUBLANES, LANES), F32) for _ in chains)
        h_fin, p_fin = lax.fori_loop(0, SCAN_CHUNK // SCAN_UNROLL, sweep_local, (zeros, ones))

        h_in = []
        for c, (d, jb) in enumerate(chains):
            first = (sub == 0) if d == 0 else (sub == SUBLANES - 1)
            shift = 1 if d == 0 else SUBLANES - 1
            init = jnp.zeros((SUBLANES, LANES), F32)
            for _ in range(SUBLANES - 1):
                true_fin = h_fin[c] + p_fin[c] * init
                init = jnp.where(first, 0.0, pltpu.roll(true_fin, shift, 0))
            h_in.append(init)

        def sweep_true(i, carry):
            hs = list(carry)
            for u in range(SCAN_UNROLL):
                for c, (d, jb) in enumerate(chains):
                    rows = chunk_rows(d, i * SCAN_UNROLL + u)
                    hs[c] = ab_ref[slot(d, 0, jb), rows, :] * hs[c] + ab_ref[slot(d, 1, jb), rows, :]
                    h_out[d][jb, chunk_rows(d, i * SCAN_UNROLL + u, HALO), :] = hs[c]
            return tuple(hs)
        lax.fori_loop(0, SCAN_CHUNK // SCAN_UNROLL, sweep_true, tuple(h_in))

        def b_out(r, c0=c0):
            rows = pl.ds(r, R)
            hrows = pl.ds(HALO + r, R)
            for jb in range(SLABS):
                gy = gy_ref[rows, c0 + jb * LANES:c0 + (jb + 1) * LANES].astype(F32)
                oc = D_CONV + c0 + jb * LANES
                o_ref[0, rows, oc:oc + LANES] = (
                    gy * (xa_ref[jb, hrows, :] + xv_ref[jb, hrows, :])).astype(BF16)
        _row_loop(L_TOT, R, b_out)


def _const_spec(block, index):
    return pl.BlockSpec(block, lambda *_: index, pipeline_mode=pl.Buffered(1))


def _heads(h, l, p):
    nb = h.shape[0]
    z2 = (l, 0, 0)
    in_specs = [
        pl.BlockSpec((1, L_TOT, D_MODEL), lambda i: (i, 0, 0)),
        _const_spec((None, D_MODEL, D_IN), z2),
        _const_spec((None, 1, D_IN), z2),
        _const_spec((None, CONV_A_WIDTH, D_CONV), z2),
        _const_spec((None, 1, D_CONV), z2),
        _const_spec((None, 1, D_CONV), z2),
        _const_spec((None, 1, D_CONV), z2),
        _const_spec((None, 2 * CONV_B_WIDTH, D_LRU), z2),
        _const_spec((None, 2, D_LRU), z2),
        _const_spec((None, 2, D_LRU // LANE_BLK, LANE_BLK, 2 * LANE_BLK), (l, 0, 0, 0, 0)),
        _const_spec((None, 2, D_LRU // LANE_BLK, 2 * LANE_BLK), (l, 0, 0, 0)),
        _const_spec((None, 2, D_LRU), z2),
        _const_spec((None, D_POOL, D_POOL), z2),
        _const_spec((None, 1, D_POOL), z2),
        _const_spec((None, 1, D_POOL), z2),
    ]
    return pl.pallas_call(
        _heads_kernel,
        grid=(nb,),
        in_specs=in_specs,
        out_specs=pl.BlockSpec((1, L_TOT, D_MODEL), lambda i: (i, 0, 0)),
        out_shape=jax.ShapeDtypeStruct((nb, L_TOT, D_MODEL), BF16),
        scratch_shapes=[
            pltpu.VMEM((SLABS, L_TOT + 2 * HALO, LANES), F32),
            pltpu.VMEM((SLABS, L_TOT + 2 * HALO, LANES), F32),
            pltpu.VMEM((2 * SLABS, L_TOT + 2 * HALO, LANES), F32),
            pltpu.VMEM((L_TOT, LANE_BLK), F32),
            pltpu.VMEM((L_TOT, 2 * LANE_BLK), F32),
            pltpu.VMEM((4 * SLABS, L_TOT, LANES), F32),
            pltpu.VMEM((L_TOT, D_LRU), BF16),
            pltpu.VMEM((ROWS_MM, D_MODEL), BF16),
        ],
        compiler_params=pltpu.CompilerParams(
            dimension_semantics=("arbitrary",), vmem_limit_bytes=HEADS_VMEM_LIMIT),
        name=f"heads{l}",
    )(h, p["w_in"], p["b_in"], p["conv_a_w"], p["conv_a_b"], p["norm_a_g"], p["norm_a_b"],
      p["conv_b_w"], p["conv_b_b"], p["w_gate"], p["gate_b"], p["lru_lambda"],
      p["pool_w"], p["pool_b"], p["pool_scale"])


def _outproj_kernel(h_ref, mix_ref, w_ref, b_ref, g_ref, beta_ref, o_ref):
    for r in range(0, L_TOT, ROWS_MM):
        for lo, hi in ((r, r + OUTPROJ_SPLIT), (r + OUTPROJ_SPLIT, r + ROWS_MM)):
            o_ref[0, lo:hi, :] = (
                ALPHA * h_ref[0, lo:hi, :]
                + jnp.dot(mix_ref[0, lo:hi, :], w_ref[...], preferred_element_type=F32) + b_ref[...])
            for t0 in range(lo, hi, ROWS_SUB):
                o_ref[0, t0:t0 + ROWS_SUB, :] = _layer_norm_1pass(
                    o_ref[0, t0:t0 + ROWS_SUB, :], g_ref[...], beta_ref[...])


def _outproj(h, mix, l, p):
    nb = h.shape[0]
    seq = lambda i: (i, 0, 0)
    return pl.pallas_call(
        _outproj_kernel,
        grid=(nb,),
        in_specs=[
            pl.BlockSpec((1, L_TOT, D_MODEL), seq),
            pl.BlockSpec((1, L_TOT, D_MODEL), seq),
            _const_spec((None, D_MODEL, D_MODEL), (l, 0, 0)),
            _const_spec((None, 1, D_MODEL), (l, 0, 0)),
            _const_spec((None, 1, D_MODEL), (l, 0, 0)),
            _const_spec((None, 1, D_MODEL), (l, 0, 0)),
        ],
        out_specs=pl.BlockSpec((1, L_TOT, D_MODEL), seq),
        out_shape=jax.ShapeDtypeStruct((nb, L_TOT, D_MODEL), F32),
        compiler_params=pltpu.CompilerParams(
            dimension_semantics=("arbitrary",), vmem_limit_bytes=VMEM_LIMIT),
        name=f"outproj{l}",
    )(h, mix, p["w_out"], p["b_out"], p["ln1_g"], p["ln1_b"])


def _ffn_kernel(h_ref, wg_ref, wv_ref, cfw_ref, cfb_ref, wd_ref, lng_ref, lnb_ref,
                o_ref, acc_ref, hb_ref, gp_ref, v_ref, p_ref, *, drop_rows):
    RM = ROWS_MM
    RS = ROWS_SUB
    T2 = 2 * RM
    n_lane_blk = FF_BLK // LANES
    j = pl.program_id(1)

    def init(r):
        for t0 in range(r, r + RM, RS):
            x = h_ref[0, t0:t0 + RS, :]
            hb_ref[t0:t0 + RS, :] = x.astype(BF16)
            acc_ref[t0:t0 + RS, :] = ALPHA * x

    def up(r):
        g = jnp.dot(hb_ref[r:r + RM, :], wg_ref[...], preferred_element_type=F32)
        for jb in range(n_lane_blk):
            gp_ref[jb, 8 + r:8 + r + RM, :] = g[:, jb * LANES:(jb + 1) * LANES]
        v_ref[r:r + RM, :] = jnp.dot(hb_ref[r:r + RM, :], wv_ref[...], preferred_element_type=F32)

    def glu_down(r):
        for t0 in range(r, r + RM, RS):
            for jb in range(n_lane_blk):
                cols = slice(jb * LANES, (jb + 1) * LANES)
                g = (cfw_ref[0:1, cols] * gp_ref[jb, t0 + 7:t0 + 7 + RS, :]
                     + cfw_ref[1:2, cols] * gp_ref[jb, t0 + 8:t0 + 8 + RS, :]
                     + cfw_ref[2:3, cols] * gp_ref[jb, t0 + 9:t0 + 9 + RS, :]
                     + cfb_ref[:, cols])
                p_ref[t0:t0 + RS, cols] = (_gelu_tanh(g) * v_ref[t0:t0 + RS, cols]).astype(BF16)
        acc_ref[r:r + RM, :] = acc_ref[r:r + RM, :] + jnp.dot(
            p_ref[r:r + RM, :], wd_ref[...], preferred_element_type=F32)

    @pl.when(j == 0)
    def _first():
        for jb in range(n_lane_blk):
            gp_ref[jb, 0:8, :] = jnp.zeros((8, LANES), F32)
            gp_ref[jb, L_TOT + 8:L_TOT + 16, :] = jnp.zeros((8, LANES), F32)
        init(0)
        up(0)
        init(RM)
        up(RM)
        glu_down(0)
        init(T2)
        up(T2)
        glu_down(RM)
        glu_down(T2)

    @pl.when(j > 0)
    def _rest():
        up(0)
        up(RM)
        glu_down(0)
        up(T2)
        glu_down(RM)
        glu_down(T2)

    @pl.when(j == N_FF_BLK - 1)
    def _finish():
        for t0 in range(drop_rows, L_TOT, RS):
            o_ref[0, t0 - drop_rows:t0 - drop_rows + RS, :] = _layer_norm_1pass(
                acc_ref[t0:t0 + RS, :], lng_ref[...], lnb_ref[...])


def _ffn(h, l, p, drop_rows):
    nb = h.shape[0]
    out_rows = L_TOT - drop_rows
    in_specs = [
        pl.BlockSpec((1, L_TOT, D_MODEL), lambda i, j: (i, 0, 0)),
        pl.BlockSpec((None, D_MODEL, FF_BLK), lambda i, j: (l, 0, j)),
        pl.BlockSpec((None, D_MODEL, FF_BLK), lambda i, j: (l, 0, j + N_FF_BLK)),
        pl.BlockSpec((None, 3, FF_BLK), lambda i, j: (l, 0, j)),
        pl.BlockSpec((None, 1, FF_BLK), lambda i, j: (l, 0, j)),
        pl.BlockSpec((None, FF_BLK, D_MODEL), lambda i, j: (l, j, 0)),
        pl.BlockSpec((None, 1, D_MODEL), lambda i, j: (l, 0, 0)),
        pl.BlockSpec((None, 1, D_MODEL), lambda i, j: (l, 0, 0)),
    ]
    return pl.pallas_call(
        functools.partial(_ffn_kernel, drop_rows=drop_rows),
        grid=(nb, N_FF_BLK),
        in_specs=in_specs,
        out_specs=pl.BlockSpec((1, out_rows, D_MODEL), lambda i, j: (i, 0, 0)),
        out_shape=jax.ShapeDtypeStruct((nb, out_rows, D_MODEL), F32),
        scratch_shapes=[
            pltpu.VMEM((L_TOT, D_MODEL), F32),
            pltpu.VMEM((L_TOT, D_MODEL), BF16),
            pltpu.VMEM((FF_BLK // LANES, L_TOT + 16, LANES), F32),
            pltpu.VMEM((L_TOT, FF_BLK), F32),
            pltpu.VMEM((L_TOT, FF_BLK), BF16),
        ],
        compiler_params=pltpu.CompilerParams(
            dimension_semantics=("arbitrary", "arbitrary"), vmem_limit_bytes=VMEM_LIMIT),
        name=f"ffn{l}",
    )(h, p["w_up"], p["w_up"], p["conv_f_w"], p["conv_f_b"], p["w_down"], p["ln2_g"], p["ln2_b"])


def _block_diag(w, n):
    d = w.shape[-1]
    eye = jnp.eye(n, dtype=w.dtype)
    full = w[..., :, :, None, :] * eye[:, None, :, None]
    return full.reshape(w.shape[:-3] + (n * d, n * d))


def _prepare_params(w_in, b_in, conv_a_w, conv_a_b, norm_a_g, norm_a_b, conv_b_w, conv_b_b,
                    gate_a_w, gate_a_b, gate_x_w, gate_x_b, lru_lambda, pool_w, pool_b, pool_scale,
                    w_out, b_out, ln1_g, ln1_b, w_up, conv_f_w, conv_f_b, w_down, ln2_g, ln2_b):
    depth = w_in.shape[0]
    heads_per_blk = LANE_BLK // (D_LRU // N_LRU_HEADS)
    n_blk = D_LRU // LANE_BLK

    def gate_blocks(w):
        w = w.reshape(depth, 2, n_blk, heads_per_blk, w.shape[-2], w.shape[-1])
        return _block_diag(w, heads_per_blk)

    row = lambda a: a.reshape(depth, 1, a.shape[-1])
    w_gate = jnp.concatenate([gate_blocks(gate_a_w), gate_blocks(gate_x_w)], axis=-1)
    bias_blocks = lambda b: b.reshape(depth, 2, n_blk, LANE_BLK)
    gate_b = 0.5 * jnp.concatenate([bias_blocks(gate_a_b), bias_blocks(gate_x_b)], axis=-1)
    return {
        "w_in": w_in.astype(BF16),
        "b_in": row(b_in),
        "conv_a_w": conv_a_w,
        "conv_a_b": row(conv_a_b),
        "norm_a_g": row(norm_a_g),
        "norm_a_b": row(norm_a_b),
        "conv_b_w": 0.5 * conv_b_w.reshape(depth, 2 * CONV_B_WIDTH, D_LRU),
        "conv_b_b": 0.5 * conv_b_b,
        "w_gate": w_gate.astype(BF16),
        "gate_b": gate_b,
        "lru_lambda": lru_lambda,
        "pool_w": _block_diag(pool_w, pool_w.shape[1]).astype(BF16),
        "pool_b": row(pool_b),
        "pool_scale": row(pool_scale),
        "w_out": w_out.astype(BF16),
        "b_out": row(b_out),
        "ln1_g": row(ln1_g),
        "ln1_b": row(ln1_b),
        "w_up": w_up.astype(BF16),
        "conv_f_w": conv_f_w,
        "conv_f_b": row(conv_f_b),
        "w_down": w_down.astype(BF16),
        "ln2_g": row(ln2_g),
        "ln2_b": row(ln2_b),
    }


def kernel(x, meta_tokens, emb_ln_g, emb_ln_b, w_in, b_in, conv_a_w, conv_a_b, norm_a_g, norm_a_b, conv_b_w, conv_b_b, gate_a_w, gate_a_b, gate_x_w, gate_x_b, lru_lambda, pool_w, pool_b, pool_scale, w_out, b_out, ln1_g, ln1_b, w_up, conv_f_w, conv_f_b, w_down, ln2_g, ln2_b):
    p = _prepare_params(w_in, b_in, conv_a_w, conv_a_b, norm_a_g, norm_a_b, conv_b_w, conv_b_b,
                        gate_a_w, gate_a_b, gate_x_w, gate_x_b, lru_lambda, pool_w, pool_b,
                        pool_scale, w_out, b_out, ln1_g, ln1_b, w_up, conv_f_w, conv_f_b, w_down,
                        ln2_g, ln2_b)
    h = _embed(x, meta_tokens, emb_ln_g.reshape(1, D_MODEL), emb_ln_b.reshape(1, D_MODEL))
    for l in range(DEPTH):
        mix = _heads(h, l, p)
        h = _outproj(h, mix, l, p)
        h = _ffn(h, l, p, drop_rows=N_META if l == DEPTH - 1 else 0)
    return h
```

```python
import functools
import math

import jax
import jax.numpy as jnp
from jax import lax
from jax.experimental import pallas as pl
from jax.experimental.pallas import tpu as pltpu

D_MODEL = 1024
SEQ = 2048
DEPTH = 4
N_META = 16
L_TOT = SEQ + N_META
D_CONV = 256
D_LRU = 512
D_POOL = 256
N_LRU_HEADS = 8
LRU_C = 8.0
CONV_A_WIDTH = 31
CONV_B_WIDTH = 4
D_FF = 2816
D_IN = 2 * D_CONV + 2 * D_LRU + D_POOL
ALPHA = (2.0 * DEPTH) ** 0.25
LN_EPS = 1e-5
GELU_C0 = math.sqrt(2.0 / math.pi)
GELU_C1 = GELU_C0 * 0.044715
LOG2_E = 1.0 / math.log(2.0)
SQRT_GUARD = 1e-30

COL_UG = 0
COL_Y = 2 * D_CONV
COL_X = 2 * D_CONV + D_LRU
COL_V = 2 * D_CONV + 2 * D_LRU

LANES = 128
SUBLANES = 8
ROWS_EW = 48
ROWS_MM = 688
ROWS_SUB = 16
OUTPROJ_SPLIT = 22 * ROWS_SUB
HALO = 16
LANE_BLK = 256
SLABS = LANE_BLK // LANES
SCAN_CHUNK = L_TOT // SUBLANES
SCAN_UNROLL = 6
FF_BLK = 256
N_FF_BLK = D_FF // FF_BLK
VMEM_LIMIT = 60 * 1024 * 1024
HEADS_VMEM_LIMIT = 62 * 1024 * 1024

F32 = jnp.float32
BF16 = jnp.bfloat16


def _row_loop(n_rows, tile, fn, first=0, last=None):
    last = n_rows // tile if last is None else last

    def body(i, c):
        fn(pl.multiple_of(i * tile, tile))
        return c
    lax.fori_loop(first, last, body, 0)


def _layer_norm(x, g, b):
    mu = jnp.mean(x, axis=-1, keepdims=True)
    xc = x - mu
    var = jnp.mean(xc * xc, axis=-1, keepdims=True)
    return xc * lax.rsqrt(var + LN_EPS) * g + b


def _layer_norm_1pass(x, g, b):
    inv_n = 1.0 / x.shape[-1]
    mu = jnp.sum(x, axis=-1, keepdims=True) * inv_n
    ex2 = jnp.sum(x * x, axis=-1, keepdims=True) * inv_n
    return (x - mu) * lax.rsqrt(ex2 - mu * mu + LN_EPS) * g + b


def _gelu_tanh(g):
    return (0.5 * g) * (1.0 + jnp.tanh(g * (GELU_C0 + GELU_C1 * (g * g))))


def _embed_kernel(x_ref, meta_ref, g_ref, b_ref, o_ref):
    g = g_ref[...]
    b = b_ref[...]
    o_ref[0, 0:N_META, :] = _layer_norm_1pass(meta_ref[...], g, b)
    for t0 in range(0, SEQ, ROWS_SUB):
        o_ref[0, N_META + t0:N_META + t0 + ROWS_SUB, :] = _layer_norm_1pass(
            x_ref[0, t0:t0 + ROWS_SUB, :], g, b)


def _embed(x, meta, g, b):
    nb = x.shape[0]
    return pl.pallas_call(
        _embed_kernel,
        grid=(nb,),
        in_specs=[
            pl.BlockSpec((1, SEQ, D_MODEL), lambda i: (i, 0, 0)),
            pl.BlockSpec((N_META, D_MODEL), lambda i: (0, 0)),
            pl.BlockSpec((1, D_MODEL), lambda i: (0, 0)),
            pl.BlockSpec((1, D_MODEL), lambda i: (0, 0)),
        ],
        out_specs=pl.BlockSpec((1, L_TOT, D_MODEL), lambda i: (i, 0, 0)),
        out_shape=jax.ShapeDtypeStruct((nb, L_TOT, D_MODEL), F32),
        compiler_params=pltpu.CompilerParams(
            dimension_semantics=("arbitrary",), vmem_limit_bytes=VMEM_LIMIT),
        name="embed",
    )(x, meta, g, b)


def _heads_kernel(h_ref, w_in_ref, b_in_ref, caw_ref, cab_ref, nag_ref, nab_ref,
                  cbw_ref, cbb_ref, wg_ref, gb_ref, lam_ref,
                  pw_ref, pb_ref, ps_ref,
                  o_ref, xa_ref, xv_ref, xx_ref, xc_ref, gt_ref, ab_ref, gy_ref, lhs_ref):
    R = ROWS_EW
    RM = ROWS_MM
    RS = ROWS_SUB
    n_tiles = L_TOT // R

    for ref in (xa_ref, xv_ref, xx_ref):
        for jb in range(ref.shape[0]):
            ref[jb, 0:HALO, :] = jnp.zeros((HALO, LANES), F32)
            ref[jb, L_TOT + HALO:L_TOT + 2 * HALO, :] = jnp.zeros((HALO, LANES), F32)

    def proj(c0, n):
        return (jnp.dot(lhs_ref[...], w_in_ref[:, c0:c0 + n], preferred_element_type=F32)
                + b_in_ref[:, c0:c0 + n])

    def to_slabs(ref, r, val):
        for jb in range(val.shape[1] // LANES):
            ref[jb, pl.ds(HALO + r, RM), :] = val[:, jb * LANES:(jb + 1) * LANES]

    def a_proj(r):
        lhs_ref[...] = h_ref[0, pl.ds(r, RM), :].astype(BF16)
        ug = proj(COL_UG, 2 * D_CONV)
        hu = 0.5 * ug[:, :D_CONV]
        to_slabs(xa_ref, r, hu + hu * jnp.tanh(0.5 * ug[:, D_CONV:]))
    _row_loop(L_TOT, RM, a_proj)

    def conv_a_rows(r, wins, t0, n):
        accs = []
        for jb, win in enumerate(wins):
            cols = slice(jb * LANES, (jb + 1) * LANES)
            acc = cab_ref[:, cols] + caw_ref[0:1, cols] * win[pl.ds(t0 + 1, n), :]
            for k in range(1, CONV_A_WIDTH):
                acc = acc + caw_ref[k:k + 1, cols] * win[pl.ds(t0 + k + 1, n), :]
            accs.append(acc)
        inv_n = 1.0 / D_CONV
        mu = sum(jnp.sum(a, axis=-1, keepdims=True) for a in accs) * inv_n
        ex2 = sum(jnp.sum(a * a, axis=-1, keepdims=True) for a in accs) * inv_n
        inv = lax.rsqrt(ex2 - mu * mu + LN_EPS)
        for jb, acc in enumerate(accs):
            cols = slice(jb * LANES, (jb + 1) * LANES)
            hy = 0.5 * ((acc - mu) * inv * nag_ref[:, cols] + nab_ref[:, cols])
            o_ref[0, pl.ds(r + t0, n), cols] = (hy + hy * jnp.tanh(hy)).astype(BF16)

    def a_conv_and_proj(r):
        lhs_ref[...] = h_ref[0, pl.ds(r, RM), :].astype(BF16)
        wins = [xa_ref.at[jb, pl.ds(r, RM + 2 * HALO), :] for jb in range(D_CONV // LANES)]

        def x_piece(half):
            val = proj(COL_X + half * LANE_BLK, LANE_BLK)
            for jb in range(SLABS):
                xx_ref[SLABS * half + jb, pl.ds(HALO + r, RM), :] = val[:, jb * LANES:(jb + 1) * LANES]

        def y_piece(half):
            cols = slice(half * LANE_BLK, (half + 1) * LANE_BLK)
            y = proj(COL_Y + half * LANE_BLK, LANE_BLK)
            gy_ref[pl.ds(r, RM), cols] = _gelu_tanh(y).astype(BF16)

        pieces = [lambda: x_piece(0), lambda: x_piece(1), lambda: y_piece(0), lambda: y_piece(1),
                  lambda: to_slabs(xv_ref, r, proj(COL_V, D_POOL))]
        groups = [(t0, min(R, RM - t0)) for t0 in range(0, RM, R)]
        bounds = [len(groups) * q // len(pieces) for q in range(len(pieces) + 1)]
        for q, piece in enumerate(pieces):
            piece()
            for t0, n in groups[bounds[q]:bounds[q + 1]]:
                conv_a_rows(r, wins, t0, n)
    _row_loop(L_TOT, RM, a_conv_and_proj)

    lane = lax.broadcasted_iota(jnp.int32, (1, LANES), 1)
    low = lane < 64

    def c_pool(r, edge):
        def taps(jb, lo, hi):
            win = xv_ref.at[jb, pl.ds(r, R + 2 * HALO), :]
            return [win[pl.ds(HALO + d, R), :] for d in range(lo, hi)]
        x0 = taps(0, -2, 2)
        s2 = x0[1] + x0[2]
        s4 = s2 + x0[0] + x0[3]
        x1 = taps(1, -8, 8)
        s8 = sum(x1[5:12], x1[4])
        s16 = sum(x1[0:4] + x1[12:16], s8)
        sums = (jnp.where(low, s2, s4), jnp.where(low, s8, s16))
        halves = ((1, 2), (4, 8))
        selfs = (x0[2], x1[8])
        for jb in range(2):
            if edge:
                t = r + lax.broadcasted_iota(jnp.int32, (R, LANES), 0)
                half = jnp.where(low, halves[jb][0], halves[jb][1])
                cnt = jnp.minimum(t + half, L_TOT) - jnp.maximum(t - half, 0)
                mean = sums[jb] / cnt.astype(F32)
            else:
                mean = sums[jb] * jnp.where(low, 0.5 / halves[jb][0], 0.5 / halves[jb][1])
            xc_ref[pl.ds(r, R), jb * LANES:(jb + 1) * LANES] = mean - selfs[jb]
    c_pool(0, True)
    _row_loop(L_TOT, R, lambda r: c_pool(r, False), first=1, last=n_tiles - 1)
    c_pool((n_tiles - 1) * R, True)

    def c_out(r):
        m = xc_ref[pl.ds(r, RM), :].astype(BF16)
        c = jnp.dot(m, pw_ref[...], preferred_element_type=F32) + pb_ref[...]
        o_ref[0, pl.ds(r, RM), D_CONV + D_LRU:D_MODEL] = (c * ps_ref[...]).astype(BF16)
    _row_loop(L_TOT, RM, c_out)

    sub = lax.broadcasted_iota(jnp.int32, (SUBLANES, LANES), 0)

    def slot(d, is_b, jb):
        return 4 * d + 2 * is_b + jb

    for cb in range(D_LRU // LANE_BLK):
        c0 = cb * LANE_BLK

        for d in range(2):
            z = -lam_ref[d:d + 1, c0:c0 + LANE_BLK]
            softplus = jnp.maximum(z, 0.0) + jnp.log1p(jnp.exp(-jnp.abs(z)))
            half_decay_log2 = (-0.5 * LRU_C * LOG2_E) * softplus

            for r in range(0, L_TOT, RM):
                for t0 in range(r, r + RM, RS):
                    for jb in range(SLABS):
                        cols = slice(c0 + jb * LANES, c0 + (jb + 1) * LANES)
                        acc = cbb_ref[d:d + 1, cols]
                        for k in range(CONV_B_WIDTH):
                            off = HALO + t0 + (k - (CONV_B_WIDTH - 1) if d == 0 else k)
                            w_k = cbw_ref[d * CONV_B_WIDTH + k:d * CONV_B_WIDTH + k + 1, cols]
                            acc = acc + w_k * xx_ref[SLABS * cb + jb, off:off + RS, :]
                        xc_ref[t0:t0 + RS, jb * LANES:(jb + 1) * LANES] = acc

                gt_ref[r:r + RM, :] = (
                    jnp.dot(xc_ref[r:r + RM, :].astype(BF16), wg_ref[d, cb], preferred_element_type=F32)
                    + gb_ref[d, cb:cb + 1, :])

                for t0 in range(r, r + RM, RS):
                    rows = slice(t0, t0 + RS)
                    for jb in range(SLABS):
                        lc = slice(jb * LANES, (jb + 1) * LANES)
                        tr = jnp.tanh(gt_ref[rows, lc])
                        ti = jnp.tanh(gt_ref[rows, LANE_BLK + jb * LANES:LANE_BLK + (jb + 1) * LANES])
                        hd = half_decay_log2[:, lc]
                        a = jnp.exp2(hd + hd * tr)
                        ab_ref[slot(d, 0, jb), rows, :] = a
                        s = 1.0 - a * a
                        ab_ref[slot(d, 1, jb), rows, :] = (
                            (s * lax.rsqrt(jnp.maximum(s, SQRT_GUARD)))
                            * ((1.0 + ti) * xc_ref[rows, lc]))

        chains = [(d, jb) for d in range(2) for jb in range(SLABS)]
        h_out = (xa_ref, xv_ref)

        def chunk_rows(d, step, base=0):
            t2 = step if d == 0 else SCAN_CHUNK - 1 - step
            return pl.ds(base + t2, SUBLANES, stride=SCAN_CHUNK)

        def sweep_local(i, carry):
            hs, ps = list(carry[0]), list(carry[1])
            for u in range(SCAN_UNROLL):
                for c, (d, jb) in enumerate(chains):
                    rows = chunk_rows(d, i * SCAN_UNROLL + u)
                    a = ab_ref[slot(d, 0, jb), rows, :]
                    hs[c] = a * hs[c] + ab_ref[slot(d, 1, jb), rows, :]
                    ps[c] = a * ps[c]
            return tuple(hs), tuple(ps)
        zeros = tuple(jnp.zeros((SUBLANES, LANES), F32) for _ in chains)
        ones = tuple(jnp.ones((SUBLANES, LANES), F32) for _ in chains)
        h_fin, p_fin = lax.fori_loop(0, SCAN_CHUNK // SCAN_UNROLL, sweep_local, (zeros, ones))

        h_in = []
        for c, (d, jb) in enumerate(chains):
            first = (sub == 0) if d == 0 else (sub == SUBLANES - 1)
            shift = 1 if d == 0 else SUBLANES - 1
            init = jnp.zeros((SUBLANES, LANES), F32)
            for _ in range(SUBLANES - 1):
                true_fin = h_fin[c] + p_fin[c] * init
                init = jnp.where(first, 0.0, pltpu.roll(true_fin, shift, 0))
            h_in.append(init)

        def sweep_true(i, carry):
            hs = list(carry)
            for u in range(SCAN_UNROLL):
                for c, (d, jb) in enumerate(chains):
                    rows = chunk_rows(d, i * SCAN_UNROLL + u)
                    hs[c] = ab_ref[slot(d, 0, jb), rows, :] * hs[c] + ab_ref[slot(d, 1, jb), rows, :]
                    h_out[d][jb, chunk_rows(d, i * SCAN_UNROLL + u, HALO), :] = hs[c]
            return tuple(hs)
        lax.fori_loop(0, SCAN_CHUNK // SCAN_UNROLL, sweep_true, tuple(h_in))

        def b_out(r, c0=c0):
            rows = pl.ds(r, R)
            hrows = pl.ds(HALO + r, R)
            for jb in range(SLABS):
                gy = gy_ref[rows, c0 + jb * LANES:c0 + (jb + 1) * LANES].astype(F32)
                oc = D_CONV + c0 + jb * LANES
                o_ref[0, rows, oc:oc + LANES] = (
                    gy * (xa_ref[jb, hrows, :] + xv_ref[jb, hrows, :])).astype(BF16)
        _row_loop(L_TOT, R, b_out)


def _const_spec(block, index):
    return pl.BlockSpec(block, lambda *_: index, pipeline_mode=pl.Buffered(1))


def _heads(h, l, p):
    nb = h.shape[0]
    z2 = (l, 0, 0)
    in_specs = [
        pl.BlockSpec((1, L_TOT, D_MODEL), lambda i: (i, 0, 0)),
        _const_spec((None, D_MODEL, D_IN), z2),
        _const_spec((None, 1, D_IN), z2),
        _const_spec((None, CONV_A_WIDTH, D_CONV), z2),
        _const_spec((None, 1, D_CONV), z2),
        _const_spec((None, 1, D_CONV), z2),
        _const_spec((None, 1, D_CONV), z2),
        _const_spec((None, 2 * CONV_B_WIDTH, D_LRU), z2),
        _const_spec((None, 2, D_LRU), z2),
        _const_spec((None, 2, D_LRU // LANE_BLK, LANE_BLK, 2 * LANE_BLK), (l, 0, 0, 0, 0)),
        _const_spec((None, 2, D_LRU // LANE_BLK, 2 * LANE_BLK), (l, 0, 0, 0)),
        _const_spec((None, 2, D_LRU), z2),
        _const_spec((None, D_POOL, D_POOL), z2),
        _const_spec((None, 1, D_POOL), z2),
        _const_spec((None, 1, D_POOL), z2),
    ]
    return pl.pallas_call(
        _heads_kernel,
        grid=(nb,),
        in_specs=in_specs,
        out_specs=pl.BlockSpec((1, L_TOT, D_MODEL), lambda i: (i, 0, 0)),
        out_shape=jax.ShapeDtypeStruct((nb, L_TOT, D_MODEL), BF16),
        scratch_shapes=[
            pltpu.VMEM((SLABS, L_TOT + 2 * HALO, LANES), F32),
            pltpu.VMEM((SLABS, L_TOT + 2 * HALO, LANES), F32),
            pltpu.VMEM((2 * SLABS, L_TOT + 2 * HALO, LANES), F32),
            pltpu.VMEM((L_TOT, LANE_BLK), F32),
            pltpu.VMEM((L_TOT, 2 * LANE_BLK), F32),
            pltpu.VMEM((4 * SLABS, L_TOT, LANES), F32),
            pltpu.VMEM((L_TOT, D_LRU), BF16),
            pltpu.VMEM((ROWS_MM, D_MODEL), BF16),
        ],
        compiler_params=pltpu.CompilerParams(
            dimension_semantics=("arbitrary",), vmem_limit_bytes=HEADS_VMEM_LIMIT),
        name=f"heads{l}",
    )(h, p["w_in"], p["b_in"], p["conv_a_w"], p["conv_a_b"], p["norm_a_g"], p["norm_a_b"],
      p["conv_b_w"], p["conv_b_b"], p["w_gate"], p["gate_b"], p["lru_lambda"],
      p["pool_w"], p["pool_b"], p["pool_scale"])


def _outproj_kernel(h_ref, mix_ref, w_ref, b_ref, g_ref, beta_ref, o_ref):
    for r in range(0, L_TOT, ROWS_MM):
        for lo, hi in ((r, r + OUTPROJ_SPLIT), (r + OUTPROJ_SPLIT, r + ROWS_MM)):
            o_ref[0, lo:hi, :] = (
                ALPHA * h_ref[0, lo:hi, :]
                + jnp.dot(mix_ref[0, lo:hi, :], w_ref[...], preferred_element_type=F32) + b_ref[...])
            for t0 in range(lo, hi, ROWS_SUB):
                o_ref[0, t0:t0 + ROWS_SUB, :] = _layer_norm_1pass(
                    o_ref[0, t0:t0 + ROWS_SUB, :], g_ref[...], beta_ref[...])


def _outproj(h, mix, l, p):
    nb = h.shape[0]
    seq = lambda i: (i, 0, 0)
    return pl.pallas_call(
        _outproj_kernel,
        grid=(nb,),
        in_specs=[
            pl.BlockSpec((1, L_TOT, D_MODEL), seq),
            pl.BlockSpec((1, L_TOT, D_MODEL), seq),
            _const_spec((None, D_MODEL, D_MODEL), (l, 0, 0)),
            _const_spec((None, 1, D_MODEL), (l, 0, 0)),
            _const_spec((None, 1, D_MODEL), (l, 0, 0)),
            _const_spec((None, 1, D_MODEL), (l, 0, 0)),
        ],
        out_specs=pl.BlockSpec((1, L_TOT, D_MODEL), seq),
        out_shape=jax.ShapeDtypeStruct((nb, L_TOT, D_MODEL), F32),
        compiler_params=pltpu.CompilerParams(
            dimension_semantics=("arbitrary",), vmem_limit_bytes=VMEM_LIMIT),
        name=f"outproj{l}",
    )(h, mix, p["w_out"], p["b_out"], p["ln1_g"], p["ln1_b"])


def _ffn_kernel(h_ref, wg_ref, wv_ref, cfw_ref, cfb_ref, wd_ref, lng_ref, lnb_ref,
                o_ref, acc_ref, hb_ref, gp_ref, v_ref, p_ref, *, drop_rows):
    RM = ROWS_MM
    RS = ROWS_SUB
    T2 = 2 * RM
    n_lane_blk = FF_BLK // LANES
    j = pl.program_id(1)

    def init(r):
        for t0 in range(r, r + RM, RS):
            x = h_ref[0, t0:t0 + RS, :]
            hb_ref[t0:t0 + RS, :] = x.astype(BF16)
            acc_ref[t0:t0 + RS, :] = ALPHA * x

    def up(r):
        g = jnp.dot(hb_ref[r:r + RM, :], wg_ref[...], preferred_element_type=F32)
        for jb in range(n_lane_blk):
            gp_ref[jb, 8 + r:8 + r + RM, :] = g[:, jb * LANES:(jb + 1) * LANES]
        v_ref[r:r + RM, :] = jnp.dot(hb_ref[r:r + RM, :], wv_ref[...], preferred_element_type=F32)

    def glu_down(r):
        for t0 in range(r, r + RM, RS):
            for jb in range(n_lane_blk):
                cols = slice(jb * LANES, (jb + 1) * LANES)
                g = (cfw_ref[0:1, cols] * gp_ref[jb, t0 + 7:t0 + 7 + RS, :]
                     + cfw_ref[1:2, cols] * gp_ref[jb, t0 + 8:t0 + 8 + RS, :]
                     + cfw_ref[2:3, cols] * gp_ref[jb, t0 + 9:t0 + 9 + RS, :]
                     + cfb_ref[:, cols])
                p_ref[t0:t0 + RS, cols] = (_gelu_tanh(g) * v_ref[t0:t0 + RS, cols]).astype(BF16)
        acc_ref[r:r + RM, :] = acc_ref[r:r + RM, :] + jnp.dot(
            p_ref[r:r + RM, :], wd_ref[...], preferred_element_type=F32)

    @pl.when(j == 0)
    def _first():
        for jb in range(n_lane_blk):
            gp_ref[jb, 0:8, :] = jnp.zeros((8, LANES), F32)
            gp_ref[jb, L_TOT + 8:L_TOT + 16, :] = jnp.zeros((8, LANES), F32)
        init(0)
        up(0)
        init(RM)
        up(RM)
        init(T2)
        up(T2)
        glu_down(0)
        glu_down(RM)
        glu_down(T2)

    @pl.when(j > 0)
    def _rest():
        up(0)
        up(RM)
        up(T2)
        glu_down(0)
        glu_down(RM)
        glu_down(T2)

    @pl.when(j == N_FF_BLK - 1)
    def _finish():
        for t0 in range(drop_rows, L_TOT, RS):
            o_ref[0, t0 - drop_rows:t0 - drop_rows + RS, :] = _layer_norm_1pass(
                acc_ref[t0:t0 + RS, :], lng_ref[...], lnb_ref[...])


def _ffn(h, l, p, drop_rows):
    nb = h.shape[0]
    out_rows = L_TOT - drop_rows
    in_specs = [
        pl.BlockSpec((1, L_TOT, D_MODEL), lambda i, j: (i, 0, 0)),
        pl.BlockSpec((None, D_MODEL, FF_BLK), lambda i, j: (l, 0, j)),
        pl.BlockSpec((None, D_MODEL, FF_BLK), lambda i, j: (l, 0, j + N_FF_BLK)),
        pl.BlockSpec((None, 3, FF_BLK), lambda i, j: (l, 0, j)),
        pl.BlockSpec((None, 1, FF_BLK), lambda i, j: (l, 0, j)),
        pl.BlockSpec((None, FF_BLK, D_MODEL), lambda i, j: (l, j, 0)),
        pl.BlockSpec((None, 1, D_MODEL), lambda i, j: (l, 0, 0)),
        pl.BlockSpec((None, 1, D_MODEL), lambda i, j: (l, 0, 0)),
    ]
    return pl.pallas_call(
        functools.partial(_ffn_kernel, drop_rows=drop_rows),
        grid=(nb, N_FF_BLK),
        in_specs=in_specs,
        out_specs=pl.BlockSpec((1, out_rows, D_MODEL), lambda i, j: (i, 0, 0)),
        out_shape=jax.ShapeDtypeStruct((nb, out_rows, D_MODEL), F32),
        scratch_shapes=[
            pltpu.VMEM((L_TOT, D_MODEL), F32),
            pltpu.VMEM((L_TOT, D_MODEL), BF16),
            pltpu.VMEM((FF_BLK // LANES, L_TOT + 16, LANES), F32),
            pltpu.VMEM((L_TOT, FF_BLK), F32),
            pltpu.VMEM((L_TOT, FF_BLK), BF16),
        ],
        compiler_params=pltpu.CompilerParams(
            dimension_semantics=("arbitrary", "arbitrary"), vmem_limit_bytes=VMEM_LIMIT),
        name=f"ffn{l}",
    )(h, p["w_up"], p["w_up"], p["conv_f_w"], p["conv_f_b"], p["w_down"], p["ln2_g"], p["ln2_b"])


def _block_diag(w, n):
    d = w.shape[-1]
    eye = jnp.eye(n, dtype=w.dtype)
    full = w[..., :, :, None, :] * eye[:, None, :, None]
    return full.reshape(w.shape[:-3] + (n * d, n * d))


def _prepare_params(w_in, b_in, conv_a_w, conv_a_b, norm_a_g, norm_a_b, conv_b_w, conv_b_b,
                    gate_a_w, gate_a_b, gate_x_w, gate_x_b, lru_lambda, pool_w, pool_b, pool_scale,
                    w_out, b_out, ln1_g, ln1_b, w_up, conv_f_w, conv_f_b, w_down, ln2_g, ln2_b):
    depth = w_in.shape[0]
    heads_per_blk = LANE_BLK // (D_LRU // N_LRU_HEADS)
    n_blk = D_LRU // LANE_BLK

    def gate_blocks(w):
        w = w.reshape(depth, 2, n_blk, heads_per_blk, w.shape[-2], w.shape[-1])
        return _block_diag(w, heads_per_blk)

    row = lambda a: a.reshape(depth, 1, a.shape[-1])
    w_gate = jnp.concatenate([gate_blocks(gate_a_w), gate_blocks(gate_x_w)], axis=-1)
    bias_blocks = lambda b: b.reshape(depth, 2, n_blk, LANE_BLK)
    gate_b = 0.5 * jnp.concatenate([bias_blocks(gate_a_b), bias_blocks(gate_x_b)], axis=-1)
    return {
        "w_in": w_in.astype(BF16),
        "b_in": row(b_in),
        "conv_a_w": conv_a_w,
        "conv_a_b": row(conv_a_b),
        "norm_a_g": row(norm_a_g),
        "norm_a_b": row(norm_a_b),
        "conv_b_w": 0.5 * conv_b_w.reshape(depth, 2 * CONV_B_WIDTH, D_LRU),
        "conv_b_b": 0.5 * conv_b_b,
        "w_gate": w_gate.astype(BF16),
        "gate_b": gate_b,
        "lru_lambda": lru_lambda,
        "pool_w": _block_diag(pool_w, pool_w.shape[1]).astype(BF16),
        "pool_b": row(pool_b),
        "pool_scale": row(pool_scale),
        "w_out": w_out.astype(BF16),
        "b_out": row(b_out),
        "ln1_g": row(ln1_g),
        "ln1_b": row(ln1_b),
        "w_up": w_up.astype(BF16),
        "conv_f_w": conv_f_w,
        "conv_f_b": row(conv_f_b),
        "w_down": w_down.astype(BF16),
        "ln2_g": row(ln2_g),
        "ln2_b": row(ln2_b),
    }


def kernel(x, meta_tokens, emb_ln_g, emb_ln_b, w_in, b_in, conv_a_w, conv_a_b, norm_a_g, norm_a_b, conv_b_w, conv_b_b, gate_a_w, gate_a_b, gate_x_w, gate_x_b, lru_lambda, pool_w, pool_b, pool_scale, w_out, b_out, ln1_g, ln1_b, w_up, conv_f_w, conv_f_b, w_down, ln2_g, ln2_b):
    p = _prepare_params(w_in, b_in, conv_a_w, conv_a_b, norm_a_g, norm_a_b, conv_b_w, conv_b_b,
                        gate_a_w, gate_a_b, gate_x_w, gate_x_b, lru_lambda, pool_w, pool_b,
                        pool_scale, w_out, b_out, ln1_g, ln1_b, w_up, conv_f_w, conv_f_b, w_down,
                        ln2_g, ln2_b)
    h = _embed(x, meta_tokens, emb_ln_g.reshape(1, D_MODEL), emb_ln_b.reshape(1, D_MODEL))
    for l in range(DEPTH):
        mix = _heads(h, l, p)
        h = _outproj(h, mix, l, p)
        h = _ffn(h, l, p, drop_rows=N_META if l == DEPTH - 1 else 0)
    return h
```
